```python
import functools
import jax, jax.numpy as jnp
from jax import lax
import numpy as np

D_MODEL = 1024
BATCH = 4
SEQ = 4096
DEPTH = 4
DEC_BATCH = 32
DEC_SEQ = 1
PAST_LEN = 8192
PAGE_SIZE = 128

D_MIX = D_MODEL
D_ATTN = D_MIX // 2
D_CONV = D_MIX - D_ATTN
HEAD_DIM = 64
N_HEADS = D_ATTN // HEAD_DIM
CONV_W = 3
Q_BLOCK = 128
RMS_EPS = 1e-6
FORGET_BIAS_INIT = 3.0
SPLIT_SIZES = (D_ATTN, D_ATTN, D_ATTN, D_ATTN, N_HEADS, D_CONV, D_CONV, D_CONV, D_CONV)
D_IN = 4 * D_ATTN + N_HEADS + 4 * D_CONV

kernel_name = "fox_shortconv_hymba_decoder_step"


def _rmsnorm(x, g):
    xf = x.astype(jnp.float32)
    xf = xf * lax.rsqrt(jnp.mean(xf * xf, axis=-1, keepdims=True) + RMS_EPS)
    return xf.astype(x.dtype) * g


def _split_proj(p):
    idx = np.cumsum(SPLIT_SIZES)[:-1].tolist()
    return jnp.split(p, idx, axis=-1)


def _fox_probs(q, k, f_q, f_k, qpos, kpos):
    s = jnp.einsum("bqhd,bkhd->bhqk", q, k).astype(jnp.float32) * (HEAD_DIM ** -0.5)
    bias = jnp.transpose(f_q, (0, 2, 1))[:, :, :, None] - jnp.transpose(f_k, (0, 2, 1))[:, :, None, :]
    s = jnp.where(kpos[None, :] <= qpos[:, None], s + bias, -jnp.inf)
    return jax.nn.softmax(s, axis=-1)


def _fox_prompt(q, k, v, logf):
    b, s, h, d = q.shape
    nblk = s // Q_BLOCK
    cum = jnp.cumsum(logf.astype(jnp.float32), axis=1)
    kpos = jnp.arange(s)
    qb = jnp.moveaxis(q.reshape(b, nblk, Q_BLOCK, h, d), 1, 0)
    cb = jnp.moveaxis(cum.reshape(b, nblk, Q_BLOCK, h), 1, 0)

    def block(args):
        i, qi, ci = args
        qpos = i * Q_BLOCK + jnp.arange(Q_BLOCK)
        p = _fox_probs(qi, k, ci, cum, qpos, kpos)
        return jnp.einsum("bhqk,bkhd->bqhd", p.astype(v.dtype), v)

    o = lax.map(block, (jnp.arange(nblk), qb, cb))
    return jnp.moveaxis(o, 0, 1).reshape(b, s, h, d)


def _fox_sample(q, k, v, logf, k_past, v_past, logf_past):
    p_len = k_past.shape[1]
    t = q.shape[1]
    kk = jnp.concatenate([k_past, k], axis=1)
    vv = jnp.concatenate([v_past, v], axis=1)
    cum = jnp.cumsum(jnp.concatenate([logf_past.astype(jnp.float32), logf.astype(jnp.float32)], axis=1), axis=1)
    qpos = p_len + jnp.arange(t)
    kpos = jnp.arange(p_len + t)
    p = _fox_probs(q, kk, cum[:, p_len:], cum, qpos, kpos)
    return jnp.einsum("bhqk,bkhd->bqhd", p.astype(vv.dtype), vv)


def _gather_pages(pool, layer, page_table):
    g = pool[layer, page_table]
    return g.reshape((g.shape[0], g.shape[1] * g.shape[2]) + g.shape[3:])


def _layer(x, c, conv_hist, attend, w_in, b_forget, w_out, norm_g, q_gain, k_gain, conv_w, conv_b, w_ada, b_ada):
    b, s, _ = x.shape
    mod = jax.nn.silu(c) @ w_ada + b_ada
    shift, scale, gate = jnp.split(mod[:, None, :], 3, axis=-1)
    h = _rmsnorm(x, norm_g) * (1 + scale) + shift
    q, k, v, za, fl, bc, cc, uc, zc = _split_proj(h @ w_in)
    q = _rmsnorm(q.reshape(b, s, N_HEADS, HEAD_DIM), q_gain)
    k = _rmsnorm(k.reshape(b, s, N_HEADS, HEAD_DIM), k_gain)
    v = v.reshape(b, s, N_HEADS, HEAD_DIM)
    logf = jax.nn.log_sigmoid((fl + b_forget).astype(jnp.float32))
    o_attn = attend(q, k, v, logf).reshape(b, s, D_ATTN) * jax.nn.silu(za)
    u = jnp.concatenate([conv_hist.astype(x.dtype), cc * uc], axis=1)
    conv = conv_b
    for j in range(CONV_W):
        conv = conv + u[:, j:j + s] * conv_w[j]
    o_conv = bc * conv * jax.nn.silu(zc)
    y = jnp.concatenate([o_attn, o_conv], axis=-1) @ w_out
    return x + gate * y, k, v, logf.astype(x.dtype), u[:, s:]


def setup_inputs(seed: int = 0) -> dict:
    key = jax.random.key(seed)
    ks = jax.random.split(key, 20)
    n_pages = PAST_LEN // PAGE_SIZE
    n_used = DEC_BATCH * n_pages
    n_phys = n_used + (n_used + 3) // 4
    page_table = jax.random.permutation(ks[0], n_phys)[:n_used].reshape(DEC_BATCH, n_pages).astype(jnp.int32)
    nrm = jax.random.normal
    return {
        "x_prompt": nrm(ks[1], (BATCH, SEQ, D_MODEL), jnp.float32),
        "x_sample": nrm(ks[2], (DEC_BATCH, DEC_SEQ, D_MODEL), jnp.float32),
        "cache_k": nrm(ks[3], (DEPTH, n_phys, PAGE_SIZE, N_HEADS, HEAD_DIM), jnp.float32),
        "cache_v": nrm(ks[4], (DEPTH, n_phys, PAGE_SIZE, N_HEADS, HEAD_DIM), jnp.float32),
        "cache_logf": jax.nn.log_sigmoid(FORGET_BIAS_INIT + 0.5 * nrm(ks[5], (DEPTH, n_phys, PAGE_SIZE, N_HEADS), jnp.float32)),
        "state_conv": 0.5 * nrm(ks[6], (DEPTH, DEC_BATCH, CONV_W - 1, D_CONV), jnp.float32),
        "page_table": page_table,
        "c_prompt": nrm(ks[7], (BATCH, D_MODEL), jnp.float32),
        "c_sample": nrm(ks[8], (DEC_BATCH, D_MODEL), jnp.float32),
        "w_in": nrm(ks[9], (DEPTH, D_MODEL, D_IN), jnp.float32) * D_MODEL ** -0.5,
        "b_forget": FORGET_BIAS_INIT + 0.1 * nrm(ks[10], (DEPTH, N_HEADS), jnp.float32),
        "w_out": nrm(ks[11], (DEPTH, D_MIX, D_MODEL), jnp.float32) * D_MIX ** -0.5,
        "norm_g": 1.0 + 0.01 * nrm(ks[12], (DEPTH, D_MODEL), jnp.float32),
        "q_gain": 1.0 + 0.01 * nrm(ks[13], (DEPTH, HEAD_DIM), jnp.float32),
        "k_gain": 1.0 + 0.01 * nrm(ks[14], (DEPTH, HEAD_DIM), jnp.float32),
        "conv_w": nrm(ks[15], (DEPTH, CONV_W, D_CONV), jnp.float32) * CONV_W ** -0.5,
        "conv_b": 0.01 * nrm(ks[16], (DEPTH, D_CONV), jnp.float32),
        "w_ada": 0.5 * nrm(ks[17], (DEPTH, D_MODEL, 3 * D_MODEL), jnp.float32) * D_MODEL ** -0.5,
        "b_ada": 0.01 * nrm(ks[18], (DEPTH, 3 * D_MODEL), jnp.float32),
    }


def reference(x_prompt, x_sample, cache_k, cache_v, cache_logf, state_conv, page_table, c_prompt, c_sample,
              w_in, b_forget, w_out, norm_g, q_gain, k_gain, conv_w, conv_b, w_ada, b_ada):
    yp, ys = x_prompt, x_sample
    kp, vp, lp, cp = [], [], [], []
    ksm, vsm, lsm, csm = [], [], [], []
    zero_hist = jnp.zeros((x_prompt.shape[0], CONV_W - 1, D_CONV), x_prompt.dtype)
    for l in range(DEPTH):
        params = (w_in[l], b_forget[l], w_out[l], norm_g[l], q_gain[l], k_gain[l],
                  conv_w[l], conv_b[l], w_ada[l], b_ada[l])
        yp, k, v, lf, hist = _layer(yp, c_prompt, zero_hist, _fox_prompt, *params)
        kp.append(k); vp.append(v); lp.append(lf); cp.append(hist)
        attend_s = functools.partial(
            _fox_sample,
            k_past=_gather_pages(cache_k, l, page_table),
            v_past=_gather_pages(cache_v, l, page_table),
            logf_past=_gather_pages(cache_logf, l, page_table))
        ys, k, v, lf, hist = _layer(ys, c_sample, state_conv[l], attend_s, *params)
        ksm.append(k); vsm.append(v); lsm.append(lf); csm.append(hist)
    return (yp, ys, jnp.stack(kp), jnp.stack(vp), jnp.stack(lp), jnp.stack(cp),
            jnp.stack(ksm), jnp.stack(vsm), jnp.stack(lsm), jnp.stack(csm))
```

```python
import functools

import jax
import jax.numpy as jnp
from jax import lax
from jax.experimental import pallas as pl
from jax.experimental.pallas import tpu as pltpu

HEAD_DIM = 64
CONV_W = 3
RMS_EPS = 1e-6
LANES = 128
SUBLANES = 8
LOG2E = 1.4426950408889634
NEG_INF = float("-inf")
PAGES_PER_CHUNK = 8

_NT = (((1,), (1,)), ((), ()))


def _bf(x):
    return x.astype(jnp.bfloat16)


def _dot(a, b):
    return jnp.dot(a, b, preferred_element_type=jnp.float32)


def _silu(x):
    return x * (1.0 / (1.0 + jnp.exp(-x)))


def _log_sigmoid(z):
    return jnp.minimum(z, 0.0) - jnp.log(1.0 + jnp.exp(-jnp.abs(z)))


def _prefix_sum_lanes(x):
    n = x.shape[-1]
    lane = lax.broadcasted_iota(jnp.int32, x.shape, x.ndim - 1)
    sh = 1
    while sh < n:
        x = x + jnp.where(lane >= sh, pltpu.roll(x, sh, x.ndim - 1), 0.0)
        sh *= 2
    return x


def _ada_kernel(c_ref, w_ref, b_ref, o_ref):
    c = c_ref[...]
    o_ref[...] = _dot(_bf(_silu(c)), _bf(w_ref[...])) + b_ref[...]


def _ada(c_all, w_ada, b_ada):
    depth, d, n3 = w_ada.shape
    rows = c_all.shape[0]
    bn = 768
    return pl.pallas_call(
        _ada_kernel,
        grid=(depth, n3 // bn),
        in_specs=[
            pl.BlockSpec((rows, d), lambda l, j: (0, 0)),
            pl.BlockSpec((None, d, bn), lambda l, j: (l, 0, j)),
            pl.BlockSpec((None, 1, bn), lambda l, j: (l, 0, j)),
        ],
        out_specs=pl.BlockSpec((None, rows, bn), lambda l, j: (l, 0, j)),
        out_shape=jax.ShapeDtypeStruct((depth, rows, n3), jnp.float32),
        name="ada_mod",
    )(c_all, w_ada, b_ada.reshape(depth, 1, n3))


def _inproj_math(x, shift, scale, g, w_ref, wf_ref, bf_row, qg, kg, bd):
    dh = qg.shape[-1]
    ms = jnp.mean(x * x, axis=-1, keepdims=True)
    xn = x * lax.rsqrt(ms + RMS_EPS)
    h = _bf(xn * g * (1.0 + scale) + shift)

    def col(c):
        return _dot(h, w_ref[:, c * dh:(c + 1) * dh])

    def head_norm(t, gain):
        t2 = _bf(t * t)
        half = dh // 2
        msq = jnp.concatenate(
            [_dot(t2[:, :half], bd), _dot(t2[:, half:], bd)], axis=1)
        return t * lax.rsqrt(msq + RMS_EPS) * gain

    q = head_norm(col(0), qg)
    k = head_norm(col(1), kg)
    v = col(2)
    sza = _silu(col(3))
    fl = _dot(h, wf_ref[...]) + bf_row
    logf = _log_sigmoid(fl)
    bc = col(4)
    cu = col(5) * col(6)
    zc = col(7)
    return q, k, v, sza, logf, bc, cu, zc


def _prompt_in_kernel(x_ref, mod_ref, g_ref, w_ref, wf_ref, bf_ref, qg_ref, kg_ref,
                      cw_ref, cb_ref, bd_ref,
                      qb_ref, kb_ref, vtb_ref, kt_ref, vt_ref, lft_ref, sza_ref, oc_ref,
                      hist_ref, carry_ref, *, d_model, n_heads, tk):
    i = pl.program_id(1)
    tm = x_ref.shape[0]
    mod = mod_ref[...]
    shift = mod[:, :d_model]
    scale = mod[:, d_model:2 * d_model]
    q, k, v, sza, logf, bc, cu, zc = _inproj_math(
        x_ref[...], shift, scale, g_ref[...], w_ref, wf_ref, bf_ref[...],
        qg_ref[...], kg_ref[...], bd_ref[...])

    qb_ref[...] = _bf(q * (LOG2E * HEAD_DIM ** -0.5))
    kb_ref[...] = _bf(k)
    kt_ref[...] = k.T
    v_t = v.T
    vt_ref[...] = v_t
    for c in range(tm // tk):
        vtb_ref[c] = _bf(v_t[:, c * tk:(c + 1) * tk])
    sza_ref[...] = sza
    lft_ref[...] = logf.T[:n_heads, :]

    @pl.when(i == 0)
    def _():
        carry_ref[...] = jnp.zeros_like(carry_ref)

    prev = carry_ref[...]
    row = lax.broadcasted_iota(jnp.int32, cu.shape, 0)
    s1 = pltpu.roll(cu, 1, 0)
    s1 = jnp.where(row == 0, prev[7:8, :], s1)
    s2 = pltpu.roll(cu, 2, 0)
    s2 = jnp.where(row == 0, prev[6:7, :], jnp.where(row == 1, prev[7:8, :], s2))
    cw = cw_ref[...]
    conv = cb_ref[...] + s2 * cw[0:1, :] + s1 * cw[1:2, :] + cu * cw[2:3, :]
    oc_ref[...] = _bf(bc * conv * _silu(zc))
    carry_ref[...] = cu[tm - SUBLANES:, :]

    @pl.when(i == pl.num_programs(1) - 1)
    def _():
        hist_ref[...] = cu[tm - (CONV_W - 1):, :]


def _prompt_in(l, x, mods, norm_g, w_main, w_f, b_f, qg, kg, conv_w, conv_b, bd, tm, tk):
    b, s, d = x.shape
    dh = qg.shape[-1]
    n_heads = dh // HEAD_DIM
    nk = s // tk
    grid = (b, s // tm)
    row_spec = lambda width: pl.BlockSpec((None, tm, width), lambda bi, i: (bi, i, 0))
    col_spec = lambda height: pl.BlockSpec((None, height, tm), lambda bi, i: (bi, 0, i))
    par = lambda shape: pl.BlockSpec((None,) + shape, lambda bi, i: (l,) + (0,) * len(shape))
    kernel = functools.partial(_prompt_in_kernel, d_model=d, n_heads=n_heads, tk=tk)
    return pl.pallas_call(
        kernel,
        grid=grid,
        in_specs=[
            row_spec(d),
            pl.BlockSpec((None, None, 1, 3 * d), lambda bi, i: (l, bi, 0, 0)),
            par((1, d)),
            pl.BlockSpec((None, d, 8 * dh), lambda bi, i: (l, 0, 0),
                         pipeline_mode=pl.Buffered(1)),
            par((d, LANES)),
            par((1, LANES)),
            par((1, dh)),
            par((1, dh)),
            par((CONV_W, dh)),
            par((1, dh)),
            pl.BlockSpec(bd.shape, lambda bi, i: (0, 0)),
        ],
        out_specs=[
            row_spec(dh),
            row_spec(dh),
            pl.BlockSpec((None, tm // tk, dh, tk), lambda bi, i: (bi, i, 0, 0)),
            col_spec(dh),
            col_spec(dh),
            col_spec(n_heads),
            row_spec(dh),
            row_spec(dh),
            pl.BlockSpec((None, CONV_W - 1, dh), lambda bi, i: (bi, 0, 0)),
        ],
        out_shape=[
            jax.ShapeDtypeStruct((b, s, dh), jnp.bfloat16),
            jax.ShapeDtypeStruct((b, s, dh), jnp.bfloat16),
            jax.ShapeDtypeStruct((b, nk, dh, tk), jnp.bfloat16),
            jax.ShapeDtypeStruct((b, dh, s), jnp.float32),
            jax.ShapeDtypeStruct((b, dh, s), jnp.float32),
            jax.ShapeDtypeStruct((b, n_heads, s), jnp.float32),
            jax.ShapeDtypeStruct((b, s, dh), jnp.float32),
            jax.ShapeDtypeStruct((b, s, dh), jnp.bfloat16),
            jax.ShapeDtypeStruct((b, CONV_W - 1, dh), jnp.float32),
        ],
        scratch_shapes=[pltpu.VMEM((SUBLANES, dh), jnp.float32)],
        compiler_params=pltpu.CompilerParams(
            dimension_semantics=("arbitrary", "arbitrary"),
            vmem_limit_bytes=56 * 1024 * 1024),
        name="prompt_in",
    )(x, mods, norm_g, w_main, w_f, b_f, qg, kg, conv_w, conv_b, bd)


def _cumsum_kernel(lf_ref, o_ref):
    o_ref[...] = _prefix_sum_lanes(lf_ref[...] * LOG2E)


def _cumsum(lf_t):
    b, h, s = lf_t.shape
    return pl.pallas_call(
        _cumsum_kernel,
        grid=(b,),
        in_specs=[pl.BlockSpec((None, h, s), lambda bi: (bi, 0, 0))],
        out_specs=pl.BlockSpec((None, h, s), lambda bi: (bi, 0, 0)),
        out_shape=jax.ShapeDtypeStruct((b, h, s), jnp.float32),
        name="forget_cumsum",
    )(lf_t)


def _attn_kernel(q_ref, k_ref, vt_ref, fk_ref, sza_ref, o_ref, *, tq, tk):
    qi = pl.program_id(2)
    q = q_ref[...]
    lane = lax.broadcasted_iota(jnp.int32, q.shape, 1)
    zero = jnp.zeros_like(q)
    q_heads = (jnp.where(lane < HEAD_DIM, q, zero), jnp.where(lane >= HEAD_DIM, q, zero))

    def tile(j, carry, masked):
        k_t = k_ref[pl.ds(pl.multiple_of(j * tk, tk), tk), :]
        new = []
        for hh in range(2):
            m, l, acc = carry[hh]
            s_t = lax.dot_general(k_t, q_heads[hh], _NT,
                                  preferred_element_type=jnp.float32)
            fk = fk_ref[pl.ds(pl.multiple_of(j * tk, tk), tk), hh:hh + 1]
            s_t = s_t - fk
            if masked:
                kpos = lax.broadcasted_iota(jnp.int32, s_t.shape, 0)
                qpos = lax.broadcasted_iota(jnp.int32, s_t.shape, 1)
                s_t = jnp.where(kpos <= qpos, s_t, NEG_INF)
            m_new = jnp.maximum(m, jnp.max(s_t, axis=0, keepdims=True))
            alpha = jnp.exp2(m - m_new)
            p = jnp.exp2(s_t - m_new)
            l = alpha * l + jnp.sum(p, axis=0, keepdims=True)
            v_t = vt_ref[j, hh * HEAD_DIM:(hh + 1) * HEAD_DIM, :]
            acc = alpha * acc + _dot(v_t, _bf(p))
            new.append((m_new, l, acc))
        return tuple(new)

    init = tuple(
        (jnp.full((1, tq), NEG_INF, jnp.float32), jnp.zeros((1, tq), jnp.float32),
         jnp.zeros((HEAD_DIM, tq), jnp.float32)) for _ in range(2))
    carry = lax.fori_loop(0, qi, lambda j, c: tile(j, c, False), init)
    carry = tile(qi, carry, True)
    o_t = jnp.concatenate([acc / l for (_, l, acc) in carry], axis=0)
    o_ref[...] = _bf(o_t.T * sza_ref[...])


def _attn(qb, kb, vtb, fk, sza, tq, tk):
    b, s, dh = qb.shape
    npair = dh // LANES
    nk = s // tk
    kernel = functools.partial(_attn_kernel, tq=tq, tk=tk)
    return pl.pallas_call(
        kernel,
        grid=(b, npair, s // tq),
        in_specs=[
            pl.BlockSpec((None, tq, LANES), lambda bi, hp, qi: (bi, qi, hp)),
            pl.BlockSpec((None, s, LANES), lambda bi, hp, qi: (bi, 0, hp)),
            pl.BlockSpec((None, nk, LANES, tk), lambda bi, hp, qi: (bi, 0, hp, 0)),
            pl.BlockSpec((None, None, s, 2), lambda bi, hp, qi: (bi, hp, 0, 0)),
            pl.BlockSpec((None, tq, LANES), lambda bi, hp, qi: (bi, qi, hp)),
        ],
        out_specs=pl.BlockSpec((None, tq, LANES), lambda bi, hp, qi: (bi, qi, hp)),
        out_shape=jax.ShapeDtypeStruct((b, s, dh), jnp.bfloat16),
        compiler_params=pltpu.CompilerParams(
            dimension_semantics=("arbitrary", "arbitrary", "arbitrary"),
            vmem_limit_bytes=48 * 1024 * 1024),
        name="prompt_attn",
    )(qb, kb, vtb, fk, sza)


def _out_kernel(x_ref, oa_ref, oc_ref, w_ref, gate_ref, y_ref):
    dh = oa_ref.shape[-1]
    y = _dot(oa_ref[...], w_ref[:dh, :]) + _dot(oc_ref[...], w_ref[dh:, :])
    y_ref[...] = x_ref[...] + gate_ref[...] * y


def _prompt_out(l, x, oa, oc, w_out, mods, tm):
    b, s, d = x.shape
    dh = oa.shape[-1]
    row_spec = lambda width: pl.BlockSpec((None, tm, width), lambda bi, i: (bi, i, 0))
    return pl.pallas_call(
        _out_kernel,
        grid=(b, s // tm),
        in_specs=[
            row_spec(d), row_spec(dh), row_spec(dh),
            pl.BlockSpec((None, 2 * dh, d), lambda bi, i: (l, 0, 0)),
            pl.BlockSpec((None, None, 1, d), lambda bi, i: (l, bi, 0, 2)),
        ],
        out_specs=row_spec(d),
        out_shape=jax.ShapeDtypeStruct((b, s, d), jnp.float32),
        compiler_params=pltpu.CompilerParams(
            dimension_semantics=("arbitrary", "arbitrary"),
            vmem_limit_bytes=48 * 1024 * 1024),
        name="prompt_out",
    )(x, oa, oc, w_out, mods)


def _sample_in_kernel(x_ref, mod_ref, st_ref, g_ref, w_ref, wf_ref, bf_ref, qg_ref, kg_ref,
                      cw_ref, cb_ref, bd_ref,
                      qt_ref, kt_ref, vt_ref, lft_ref, k_ref, v_ref, lf_ref, sza_ref, oc_ref,
                      hist_ref, *, d_model, n_heads):
    mod = mod_ref[...]
    shift = mod[:, :d_model]
    scale = mod[:, d_model:2 * d_model]
    q, k, v, sza, logf, bc, cu, zc = _inproj_math(
        x_ref[...], shift, scale, g_ref[...], w_ref, wf_ref, bf_ref[...],
        qg_ref[...], kg_ref[...], bd_ref[...])
    dh = q.shape[-1]
    qt_ref[...] = (q * (LOG2E * HEAD_DIM ** -0.5)).T
    kt_ref[...] = k.T
    vt_ref[...] = v.T
    lft_ref[...] = (logf * LOG2E).T[:SUBLANES, :]
    k_ref[...] = k
    v_ref[...] = v
    lf_ref[...] = logf
    sza_ref[...] = sza
    st = st_ref[...]
    h0 = st[:, :dh]
    h1 = st[:, dh:]
    cw = cw_ref[...]
    conv = cb_ref[...] + h0 * cw[0:1, :] + h1 * cw[1:2, :] + cu * cw[2:3, :]
    oc_ref[...] = _bf(bc * conv * _silu(zc))
    hist_ref[...] = jnp.concatenate([h1, cu], axis=1)


def _sample_in(l, x, mod_s, st, norm_g, w_main, w_f, b_f, qg, kg, conv_w, conv_b, bd):
    r, d = x.shape
    dh = qg.shape[-1]
    n_heads = dh // HEAD_DIM
    full = lambda shape: pl.BlockSpec(shape, lambda i: (0,) * len(shape))
    par = lambda shape: pl.BlockSpec((None,) + shape, lambda i: (l,) + (0,) * len(shape))
    kernel = functools.partial(_sample_in_kernel, d_model=d, n_heads=n_heads)
    f32 = jnp.float32
    return pl.pallas_call(
        kernel,
        grid=(1,),
        in_specs=[
            full((r, d)), par((r, 3 * d)), par((r, 2 * dh)), par((1, d)),
            par((d, 8 * dh)), par((d, LANES)), par((1, LANES)), par((1, dh)), par((1, dh)),
            par((CONV_W, dh)), par((1, dh)), full(bd.shape),
        ],
        out_specs=[
            full((dh, r)), full((dh, r)), full((dh, r)), full((SUBLANES, r)),
            full((r, dh)), full((r, dh)), full((r, LANES)), full((r, dh)), full((r, dh)),
            full((r, 2 * dh)),
        ],
        out_shape=[
            jax.ShapeDtypeStruct((dh, r), f32),
            jax.ShapeDtypeStruct((dh, r), f32),
            jax.ShapeDtypeStruct((dh, r), f32),
            jax.ShapeDtypeStruct((SUBLANES, r), f32),
            jax.ShapeDtypeStruct((r, dh), f32),
            jax.ShapeDtypeStruct((r, dh), f32),
            jax.ShapeDtypeStruct((r, LANES), f32),
            jax.ShapeDtypeStruct((r, dh), f32),
            jax.ShapeDtypeStruct((r, dh), jnp.bfloat16),
            jax.ShapeDtypeStruct((r, 2 * dh), f32),
        ],
        compiler_params=pltpu.CompilerParams(vmem_limit_bytes=56 * 1024 * 1024),
        name="sample_in",
    )(x, mod_s, st, norm_g, w_main, w_f, b_f, qg, kg, conv_w, conv_b, bd)


def _sample_attn_kernel(pt_ref, qt_ref, ktn_ref, vtn_ref, lfn_ref, k_hbm, v_hbm, lf_hbm,
                        ot_ref, kbuf, vbuf, lfbuf, sems, qb_ref, acc_ref, m_ref, l_ref, c_ref,
                        *, layer_off, n_pages, n_heads, n_seq):
    cpp = kbuf.shape[1]
    nch = n_pages // cpp
    g = pl.program_id(0)
    nsteps = pl.num_programs(0)
    slot = lax.rem(g, 2)
    seq = g // nch
    ch = lax.rem(g, nch)

    def copies(step, sl):
        sq = step // nch
        c0 = lax.rem(step, nch) * cpp
        out = []
        for p in range(cpp):
            pid = pt_ref[sq * n_pages + c0 + p] + layer_off
            out.append(pltpu.make_async_copy(k_hbm.at[pid], kbuf.at[sl, p], sems.at[sl, 0]))
            out.append(pltpu.make_async_copy(v_hbm.at[pid], vbuf.at[sl, p], sems.at[sl, 1]))
            out.append(pltpu.make_async_copy(lf_hbm.at[pid], lfbuf.at[sl, p], sems.at[sl, 2]))
        return out

    @pl.when(g == 0)
    def _():
        ot_ref[...] = jnp.zeros_like(ot_ref)
        for cp in copies(0, 0):
            cp.start()

    @pl.when(g + 1 < nsteps)
    def _():
        for cp in copies(g + 1, 1 - slot):
            cp.start()

    for cp in copies(g, slot):
        cp.wait()

    lane_r = lax.broadcasted_iota(jnp.int32, qt_ref.shape, 1)

    def column(ref):
        return jnp.sum(jnp.where(lane_r == seq, ref[...], 0.0), axis=1, keepdims=True)

    @pl.when(ch == 0)
    def _():
        qcol = column(qt_ref)
        qb_ref[...] = jnp.broadcast_to(qcol, qb_ref.shape)
        acc_ref[...] = jnp.zeros_like(acc_ref)
        m_ref[...] = jnp.full_like(m_ref, NEG_INF)
        l_ref[...] = jnp.zeros_like(l_ref)
        c_ref[...] = jnp.zeros_like(c_ref)

    rows = []
    for h in range(n_heads):
        qh = qb_ref[h * HEAD_DIM:(h + 1) * HEAD_DIM, :]
        parts = [jnp.sum(kbuf[slot, p, h] * qh, axis=0, keepdims=True) for p in range(cpp)]
        rows.append(jnp.concatenate(parts, axis=1))
    s = jnp.concatenate(rows, axis=0)
    lf = jnp.concatenate([lfbuf[slot, p] for p in range(cpp)], axis=1) * LOG2E
    cum = _prefix_sum_lanes(lf) + c_ref[...]
    n_tok = s.shape[1]
    c_ref[...] = cum[:, n_tok - 1:n_tok]
    logit = s - cum
    m_old = m_ref[...]
    m_new = jnp.maximum(m_old, jnp.max(logit, axis=1, keepdims=True))
    alpha = jnp.exp2(m_old - m_new)
    p_t = jnp.exp2(logit - m_new)
    l_ref[...] = alpha * l_ref[...] + jnp.sum(p_t, axis=1, keepdims=True)
    m_ref[...] = m_new
    for h in range(n_heads):
        a = acc_ref[h * HEAD_DIM:(h + 1) * HEAD_DIM, :] * alpha[h:h + 1, :]
        for p in range(cpp):
            a = a + vbuf[slot, p, h] * p_t[h:h + 1, p * LANES:(p + 1) * LANES]
        acc_ref[h * HEAD_DIM:(h + 1) * HEAD_DIM, :] = a

    @pl.when(ch == nch - 1)
    def _():
        qcol = qb_ref[:, 0:1]
        kcol = column(ktn_ref)
        vcol = column(vtn_ref)
        lane8 = lax.broadcasted_iota(jnp.int32, lfn_ref.shape, 1)
        lfcol = jnp.sum(jnp.where(lane8 == seq, lfn_ref[...], 0.0), axis=1, keepdims=True)
        cum_self = c_ref[...] + lfcol
        qk = qcol * kcol
        m_old = m_ref[...]
        l_old = l_ref[...]
        cols = []
        for h in range(n_heads):
            sl_h = slice(h * HEAD_DIM, (h + 1) * HEAD_DIM)
            s_self = jnp.sum(qk[sl_h, :], axis=0, keepdims=True) - cum_self[h:h + 1, :]
            m_h = m_old[h:h + 1, :]
            m_fin = jnp.maximum(m_h, s_self)
            a_h = jnp.exp2(m_h - m_fin)
            p_self = jnp.exp2(s_self - m_fin)
            l_fin = a_h * l_old[h:h + 1, :] + p_self
            o_h = a_h * jnp.sum(acc_ref[sl_h, :], axis=1, keepdims=True) + p_self * vcol[sl_h, :]
            cols.append(o_h / l_fin)
        ocol = jnp.concatenate(cols, axis=0)
        ot_ref[...] = jnp.where(lane_r == seq, ocol, ot_ref[...])


def _sample_attn(layer_off, page_table, qt, ktn, vtn, lfn, ck, cv, clf, n_seq):
    dh, r = qt.shape
    n_heads = dh // HEAD_DIM
    n_pages = page_table.shape[1]
    cpp = min(PAGES_PER_CHUNK, n_pages)
    nch = n_pages // cpp
    page = ck.shape[-1]
    kernel = functools.partial(
        _sample_attn_kernel, layer_off=layer_off, n_pages=n_pages, n_heads=n_heads, n_seq=n_seq)
    full = lambda shape: pl.BlockSpec(shape, lambda g, pt: (0,) * len(shape))
    anyspec = pl.BlockSpec(memory_space=pl.ANY)
    grid_spec = pltpu.PrefetchScalarGridSpec(
        num_scalar_prefetch=1,
        grid=(n_seq * nch,),
        in_specs=[full((dh, r)), full((dh, r)), full((dh, r)), full((SUBLANES, r)),
                  anyspec, anyspec, anyspec],
        out_specs=full((dh, r)),
        scratch_shapes=[
            pltpu.VMEM((2, cpp, n_heads, HEAD_DIM, page), jnp.float32),
            pltpu.VMEM((2, cpp, n_heads, HEAD_DIM, page), jnp.float32),
            pltpu.VMEM((2, cpp, n_heads, page), jnp.float32),
            pltpu.SemaphoreType.DMA((2, 3)),
            pltpu.VMEM((dh, page), jnp.float32),
            pltpu.VMEM((dh, page), jnp.float32),
            pltpu.VMEM((n_heads, 1), jnp.float32),
            pltpu.VMEM((n_heads, 1), jnp.float32),
            pltpu.VMEM((n_heads, 1), jnp.float32),
        ],
    )
    return pl.pallas_call(
        kernel,
        grid_spec=grid_spec,
        out_shape=jax.ShapeDtypeStruct((dh, r), jnp.float32),
        compiler_params=pltpu.CompilerParams(
            dimension_semantics=("arbitrary",),
            vmem_limit_bytes=48 * 1024 * 1024),
        name="sample_attn",
    )(page_table.reshape(-1), qt, ktn, vtn, lfn, ck, cv, clf)


def _sample_out_kernel(x_ref, ot_ref, sza_ref, oc_ref, w_ref, mod_ref, y_ref, *, d_model):
    dh = oc_ref.shape[-1]
    oa = _bf(ot_ref[...].T * sza_ref[...])
    y = _dot(oa, w_ref[:dh, :]) + _dot(oc_ref[...], w_ref[dh:, :])
    gate = mod_ref[:, 2 * d_model:]
    y_ref[...] = x_ref[...] + gate * y


def _sample_out(l, x, ot, sza, oc, w_out, mod_s):
    r, d = x.shape
    dh = oc.shape[-1]
    full = lambda shape: pl.BlockSpec(shape, lambda i: (0,) * len(shape))
    par = lambda shape: pl.BlockSpec((None,) + shape, lambda i: (l,) + (0,) * len(shape))
    return pl.pallas_call(
        functools.partial(_sample_out_kernel, d_model=d),
        grid=(1,),
        in_specs=[full((r, d)), full((dh, r)), full((r, dh)), full((r, dh)),
                  par((2 * dh, d)), par((r, 3 * d))],
        out_specs=full((r, d)),
        out_shape=jax.ShapeDtypeStruct((r, d), jnp.float32),
        name="sample_out",
    )(x, ot, sza, oc, w_out, mod_s)


def kernel(x_prompt, x_sample, cache_k, cache_v, cache_logf, state_conv, page_table, c_prompt, c_sample,
           w_in, b_forget, w_out, norm_g, q_gain, k_gain, conv_w, conv_b, w_ada, b_ada):
    depth, d, _ = w_in.shape
    b, s, _ = x_prompt.shape
    n_seq = x_sample.shape[0]
    n_heads = b_forget.shape[-1]
    dh = n_heads * HEAD_DIM
    n_phys, page = cache_k.shape[1], cache_k.shape[2]
    tm = min(512, s)
    tq = tk = min(256, s)
    r_pad = LANES

    f0 = 4 * dh
    w_main = _bf(jnp.concatenate([w_in[:, :, :f0], w_in[:, :, f0 + n_heads:]], axis=-1))
    w_f = _bf(jnp.pad(w_in[:, :, f0:f0 + n_heads], ((0, 0), (0, 0), (0, LANES - n_heads))))
    b_f = jnp.pad(b_forget, ((0, 0), (0, LANES - n_heads))).reshape(depth, 1, LANES)
    w_out_b = _bf(w_out)
    qg = jnp.tile(q_gain, (1, n_heads)).reshape(depth, 1, dh)
    kg = jnp.tile(k_gain, (1, n_heads)).reshape(depth, 1, dh)
    g3 = norm_g.reshape(depth, 1, d)
    cb3 = conv_b.reshape(depth, 1, dh)
    half = dh // 2
    hid = jnp.arange(half) // HEAD_DIM
    bd = _bf(jnp.where(hid[:, None] == hid[None, :], 1.0 / HEAD_DIM, 0.0))

    n_prompt = c_prompt.shape[0]
    c_all = jnp.concatenate([c_prompt, c_sample], axis=0)
    rows = c_all.shape[0]
    rows_pad = -(-rows // SUBLANES) * SUBLANES
    c_all = jnp.pad(c_all, ((0, rows_pad - rows), (0, 0)))
    mods = _ada(c_all, w_ada, b_ada)
    mods4 = mods.reshape(depth, rows_pad, 1, 3 * d)
    mod_s = jnp.pad(mods[:, n_prompt:n_prompt + n_seq, :], ((0, 0), (0, r_pad - n_seq), (0, 0)))

    ck = jnp.transpose(cache_k, (0, 1, 3, 4, 2)).reshape(depth * n_phys, n_heads, HEAD_DIM, page)
    cv = jnp.transpose(cache_v, (0, 1, 3, 4, 2)).reshape(depth * n_phys, n_heads, HEAD_DIM, page)
    clf = jnp.transpose(cache_logf, (0, 1, 3, 2)).reshape(depth * n_phys, n_heads, page)
    st_all = jnp.pad(state_conv.reshape(depth, n_seq, (CONV_W - 1) * dh),
                     ((0, 0), (0, r_pad - n_seq), (0, 0)))

    yp = x_prompt
    kp, vp, lp, cp = [], [], [], []
    for l in range(depth):
        qb, kb, vtb, kt, vt, lft, sza, oc, hist = _prompt_in(
            l, yp, mods4, g3, w_main, w_f, b_f, qg, kg, conv_w, cb3, bd, tm, tk)
        f2 = _cumsum(lft)
        fk = jnp.transpose(f2.reshape(b, n_heads // 2, 2, s), (0, 1, 3, 2))
        oa = _attn(qb, kb, vtb, fk, sza, tq, tk)
        yp = _prompt_out(l, yp, oa, oc, w_out_b, mods4, min(1024, s))
        kp.append(jnp.transpose(kt.reshape(b, n_heads, HEAD_DIM, s), (0, 3, 1, 2)))
        vp.append(jnp.transpose(vt.reshape(b, n_heads, HEAD_DIM, s), (0, 3, 1, 2)))
        lp.append(jnp.transpose(lft, (0, 2, 1)))
        cp.append(hist)

    ys = jnp.pad(x_sample.reshape(n_seq, d), ((0, r_pad - n_seq), (0, 0)))
    ksm, vsm, lsm, csm = [], [], [], []
    for l in range(depth):
        qt, ktn, vtn, lfn, k, v, lf, sza, oc, hist = _sample_in(
            l, ys, mod_s, st_all, g3, w_main, w_f, b_f, qg, kg, conv_w, cb3, bd)
        ot = _sample_attn(l * n_phys, page_table, qt, ktn, vtn, lfn, ck, cv, clf, n_seq)
        ys = _sample_out(l, ys, ot, sza, oc, w_out_b, mod_s)
        ksm.append(k[:n_seq].reshape(n_seq, 1, n_heads, HEAD_DIM))
        vsm.append(v[:n_seq].reshape(n_seq, 1, n_heads, HEAD_DIM))
        lsm.append(lf[:n_seq, :n_heads].reshape(n_seq, 1, n_heads))
        csm.append(hist[:n_seq].reshape(n_seq, CONV_W - 1, dh))

    return (yp, ys[:n_seq].reshape(n_seq, 1, d), jnp.stack(kp), jnp.stack(vp), jnp.stack(lp),
            jnp.stack(cp), jnp.stack(ksm), jnp.stack(vsm), jnp.stack(lsm), jnp.stack(csm))
```

```python
import functools

import jax
import jax.numpy as jnp
from jax import lax
from jax.experimental import pallas as pl
from jax.experimental.pallas import tpu as pltpu

HEAD_DIM = 64
CONV_W = 3
RMS_EPS = 1e-6
LANES = 128
SUBLANES = 8
BF16_ROWS = 16
LOG2E = 1.4426950408889634
NEG_INF = float("-inf")
PAGES_PER_CHUNK = 16
EXP_ROWS = 128

_NT = (((1,), (1,)), ((), ()))


def _bf(x):
    return x.astype(jnp.bfloat16)


def _dot(a, b):
    return jnp.dot(a, b, preferred_element_type=jnp.float32)


def _dot_nt(a, b):
    return lax.dot_general(a, b, _NT, preferred_element_type=jnp.float32)


def _silu(x):
    return x * (1.0 / (1.0 + jnp.exp(-x)))


def _log_sigmoid(z):
    return jnp.minimum(z, 0.0) - jnp.log(1.0 + jnp.exp(-jnp.abs(z)))


def _prefix_sum_lanes(x):
    n = x.shape[-1]
    lane = lax.broadcasted_iota(jnp.int32, x.shape, x.ndim - 1)
    sh = 1
    while sh < n:
        x = x + jnp.where(lane >= sh, pltpu.roll(x, sh, x.ndim - 1), 0.0)
        sh *= 2
    return x


def _ada_kernel(c_ref, w_ref, b_ref, o_ref):
    c = c_ref[...]
    o_ref[...] = _dot(_bf(_silu(c)), _bf(w_ref[...])) + b_ref[...]


def _ada(c_all, w_ada, b_ada):
    depth, d, n3 = w_ada.shape
    rows = c_all.shape[0]
    bn = 768
    return pl.pallas_call(
        _ada_kernel,
        grid=(depth, n3 // bn),
        in_specs=[
            pl.BlockSpec((rows, d), lambda l, j: (0, 0)),
            pl.BlockSpec((None, d, bn), lambda l, j: (l, 0, j)),
            pl.BlockSpec((None, 1, bn), lambda l, j: (l, 0, j)),
        ],
        out_specs=pl.BlockSpec((None, rows, bn), lambda l, j: (l, 0, j)),
        out_shape=jax.ShapeDtypeStruct((depth, rows, n3), jnp.float32),
        name="ada_mod",
    )(c_all, w_ada, b_ada.reshape(depth, 1, n3))


def _inproj_math(x, shift, scale, g, w_ref, wf_ref, bf_row, qg, kg, bd):
    dh = qg.shape[-1]
    ms = jnp.mean(x * x, axis=-1, keepdims=True)
    xn = x * lax.rsqrt(ms + RMS_EPS)
    h = _bf(xn * g * (1.0 + scale) + shift)

    def col(c):
        return _dot(h, w_ref[:, c * dh:(c + 1) * dh])

    def head_norm(t, gain):
        t2 = _bf(t * t)
        half = dh // 2
        msq = jnp.concatenate(
            [_dot(t2[:, :half], bd), _dot(t2[:, half:], bd)], axis=1)
        return t * lax.rsqrt(msq + RMS_EPS) * gain

    q = head_norm(col(0), qg)
    k = head_norm(col(1), kg)
    v = col(2)
    sza = _silu(col(3))
    fl = _dot(h, wf_ref[...]) + bf_row
    logf = _log_sigmoid(fl)
    bc = col(4)
    cu = col(5) * col(6)
    zc = col(7)
    return q, k, v, sza, logf, bc, cu, zc


def _prompt_in_kernel(x_ref, mod_ref, g_ref, w_ref, wf_ref, bf_ref, qg_ref, kg_ref,
                      cw_ref, cb_ref, bd_ref,
                      qb_ref, kb_ref, vtb_ref, kt_ref, vt_ref, lft_ref, sza_ref, oc_ref,
                      hist_ref, carry_ref, *, d_model, n_heads, tk):
    i = pl.program_id(1)
    tm = x_ref.shape[0]
    mod = mod_ref[...]
    shift = mod[:, :d_model]
    scale = mod[:, d_model:2 * d_model]
    q, k, v, sza, logf, bc, cu, zc = _inproj_math(
        x_ref[...], shift, scale, g_ref[...], w_ref, wf_ref, bf_ref[...],
        qg_ref[...], kg_ref[...], bd_ref[...])

    qb_ref[...] = _bf(q * (LOG2E * HEAD_DIM ** -0.5))
    kb_ref[...] = _bf(k)
    kt_ref[...] = k.T
    v_t = v.T
    vt_ref[...] = v_t
    for c in range(tm // tk):
        vtb_ref[c] = _bf(v_t[:, c * tk:(c + 1) * tk])
    sza_ref[...] = sza
    lft_ref[...] = logf.T[:n_heads, :]

    @pl.when(i == 0)
    def _():
        carry_ref[...] = jnp.zeros_like(carry_ref)

    prev = carry_ref[...]
    row = lax.broadcasted_iota(jnp.int32, cu.shape, 0)
    s1 = pltpu.roll(cu, 1, 0)
    s1 = jnp.where(row == 0, prev[7:8, :], s1)
    s2 = pltpu.roll(cu, 2, 0)
    s2 = jnp.where(row == 0, prev[6:7, :], jnp.where(row == 1, prev[7:8, :], s2))
    cw = cw_ref[...]
    conv = cb_ref[...] + s2 * cw[0:1, :] + s1 * cw[1:2, :] + cu * cw[2:3, :]
    oc_ref[...] = _bf(bc * conv * _silu(zc))
    carry_ref[...] = cu[tm - SUBLANES:, :]

    @pl.when(i == pl.num_programs(1) - 1)
    def _():
        hist_ref[...] = cu[tm - (CONV_W - 1):, :]


def _prompt_in(l, x, mods, norm_g, w_main, w_f, b_f, qg, kg, conv_w, conv_b, bd, tm, tk):
    b, s, d = x.shape
    dh = qg.shape[-1]
    n_heads = dh // HEAD_DIM
    nk = s // tk
    grid = (b, s // tm)
    row_spec = lambda width: pl.BlockSpec((None, tm, width), lambda bi, i: (bi, i, 0))
    col_spec = lambda height: pl.BlockSpec((None, height, tm), lambda bi, i: (bi, 0, i))
    par = lambda shape: pl.BlockSpec((None,) + shape, lambda bi, i: (l,) + (0,) * len(shape))
    kernel = functools.partial(_prompt_in_kernel, d_model=d, n_heads=n_heads, tk=tk)
    return pl.pallas_call(
        kernel,
        grid=grid,
        in_specs=[
            row_spec(d),
            pl.BlockSpec((None, None, 1, 3 * d), lambda bi, i: (l, bi, 0, 0)),
            par((1, d)),
            pl.BlockSpec((None, d, 8 * dh), lambda bi, i: (l, 0, 0),
                         pipeline_mode=pl.Buffered(1)),
            par((d, LANES)),
            par((1, LANES)),
            par((1, dh)),
            par((1, dh)),
            par((CONV_W, dh)),
            par((1, dh)),
            pl.BlockSpec(bd.shape, lambda bi, i: (0, 0)),
        ],
        out_specs=[
            row_spec(dh),
            row_spec(dh),
            pl.BlockSpec((None, tm // tk, dh, tk), lambda bi, i: (bi, i, 0, 0)),
            col_spec(dh),
            col_spec(dh),
            col_spec(n_heads),
            row_spec(dh),
            row_spec(dh),
            pl.BlockSpec((None, CONV_W - 1, dh), lambda bi, i: (bi, 0, 0)),
        ],
        out_shape=[
            jax.ShapeDtypeStruct((b, s, dh), jnp.bfloat16),
            jax.ShapeDtypeStruct((b, s, dh), jnp.bfloat16),
            jax.ShapeDtypeStruct((b, nk, dh, tk), jnp.bfloat16),
            jax.ShapeDtypeStruct((b, dh, s), jnp.float32),
            jax.ShapeDtypeStruct((b, dh, s), jnp.float32),
            jax.ShapeDtypeStruct((b, n_heads, s), jnp.float32),
            jax.ShapeDtypeStruct((b, s, dh), jnp.float32),
            jax.ShapeDtypeStruct((b, s, dh), jnp.bfloat16),
            jax.ShapeDtypeStruct((b, CONV_W - 1, dh), jnp.float32),
        ],
        scratch_shapes=[pltpu.VMEM((SUBLANES, dh), jnp.float32)],
        compiler_params=pltpu.CompilerParams(
            dimension_semantics=("arbitrary", "arbitrary"),
            vmem_limit_bytes=56 * 1024 * 1024),
        name="prompt_in",
    )(x, mods, norm_g, w_main, w_f, b_f, qg, kg, conv_w, conv_b, bd)


def _cumsum_kernel(lf_ref, hi_ref, mid_ref, lo_ref):
    x = -_prefix_sum_lanes(lf_ref[...] * LOG2E)
    hi = _bf(x).astype(jnp.float32)
    r = x - hi
    mid = _bf(r).astype(jnp.float32)
    hi_ref[...] = hi
    mid_ref[...] = mid
    lo_ref[...] = r - mid


def _cumsum(lf_t):
    b, h, s = lf_t.shape
    spec = pl.BlockSpec((None, h, s), lambda bi: (bi, 0, 0))
    shape = jax.ShapeDtypeStruct((b, h, s), jnp.float32)
    return pl.pallas_call(
        _cumsum_kernel,
        grid=(b,),
        in_specs=[spec],
        out_specs=[spec, spec, spec],
        out_shape=[shape, shape, shape],
        name="forget_cumsum",
    )(lf_t)


def _attn_kernel(q_ref, k_ref, ka_ref, vt_ref, sza_ref, o_ref,
                 s_scr, tmax_scr, m_scr, acc_scr, *, tq, tk):
    qi = pl.program_id(2)
    q = q_ref[...]
    lane = lax.broadcasted_iota(jnp.int32, q.shape, 1)
    zero = jnp.zeros_like(q)
    q_split = (jnp.where(lane < HEAD_DIM, q, zero), jnp.where(lane >= HEAD_DIM, q, zero))
    q_heads = []
    for hh in range(2):
        bias_ones = _bf(jnp.where((lane >= 3 * hh) & (lane < 3 * hh + 3), 1.0, 0.0))
        q_heads.append(jnp.concatenate([q_split[hh], bias_ones], axis=1))
    ones_rows = jnp.ones((BF16_ROWS, tk), jnp.bfloat16)

    m_scr[...] = jnp.full_like(m_scr, NEG_INF)
    acc_scr[...] = jnp.zeros_like(acc_scr)

    def qk(j):
        rows = pl.ds(pl.multiple_of(j * tk, tk), tk)
        k_t = jnp.concatenate([k_ref[rows, :], ka_ref[rows, :]], axis=1)
        return [_dot_nt(k_t, q_heads[hh]) for hh in range(2)]

    def scores_to_scratch(s_pair, masked, slot):
        for hh in range(2):
            s_t = s_pair[hh]
            if masked:
                kpos = lax.broadcasted_iota(jnp.int32, s_t.shape, 0)
                qpos = lax.broadcasted_iota(jnp.int32, s_t.shape, 1)
                s_t = jnp.where(kpos <= qpos, s_t, NEG_INF)
            s_scr[slot, hh] = s_t
            col_max = jnp.max(s_t, axis=0, keepdims=True)
            tmax_scr[slot, hh] = jnp.broadcast_to(col_max, (SUBLANES, tq))

    def rep(stat, rows):
        return jnp.broadcast_to(stat[None], (rows // SUBLANES, SUBLANES, tq)).reshape(rows, tq)

    def exps(slot):
        out = []
        for hh in range(2):
            m_old = m_scr[hh]
            m_new = jnp.maximum(m_old, tmax_scr[slot, hh])
            alpha = jnp.exp2(m_old - m_new)
            m_scr[hh] = m_new
            m_rep = rep(m_new, EXP_ROWS)
            p = jnp.concatenate(
                [_bf(jnp.exp2(s_scr[slot, hh, c * EXP_ROWS:(c + 1) * EXP_ROWS, :] - m_rep))
                 for c in range(tk // EXP_ROWS)], axis=0)
            out.append((alpha, p))
        return out

    def pv(j, ap):
        out = []
        for hh in range(2):
            alpha, p = ap[hh]
            v_t = jnp.concatenate(
                [vt_ref[j, hh * HEAD_DIM:(hh + 1) * HEAD_DIM, :], ones_rows], axis=0)
            out.append((alpha, _dot(v_t, p)))
        return out

    def accumulate(av):
        for hh in range(2):
            alpha, pv_t = av[hh]
            acc_scr[hh] = rep(alpha, pv_t.shape[0]) * acc_scr[hh] + pv_t

    def step(j_next, masked, slot_next):
        ap = exps(1 - slot_next)
        s_next = qk(j_next)
        av = pv(j_next - 1, ap)
        scores_to_scratch(s_next, masked, slot_next)
        accumulate(av)

    @pl.when(qi == 0)
    def _():
        scores_to_scratch(qk(0), True, 0)

    @pl.when(qi > 0)
    def _():
        scores_to_scratch(qk(0), False, 0)

    n_plain = jnp.maximum(qi - 1, 0)

    def body(jj, carry):
        step(2 * jj + 1, False, 1)
        step(2 * jj + 2, False, 0)
        return carry

    lax.fori_loop(0, n_plain // 2, body, 0)

    @pl.when(lax.rem(n_plain, 2) == 1)
    def _():
        step(qi - 1, False, 1)

    last_slot = lax.rem(qi, 2)

    @pl.when(qi > 0)
    def _():
        step(qi, True, last_slot)

    accumulate(pv(qi, exps(last_slot)))
    o_t = jnp.concatenate(
        [acc_scr[hh, :HEAD_DIM, :] / acc_scr[hh, HEAD_DIM:HEAD_DIM + 1, :] for hh in range(2)],
        axis=0)
    o_ref[...] = _bf(o_t.T * sza_ref[...])


def _attn(qb, kb, ka, vtb, sza, tq, tk):
    b, s, dh = qb.shape
    npair = dh // LANES
    nk = s // tk
    kernel = functools.partial(_attn_kernel, tq=tq, tk=tk)
    return pl.pallas_call(
        kernel,
        grid=(b, npair, s // tq),
        in_specs=[
            pl.BlockSpec((None, tq, LANES), lambda bi, hp, qi: (bi, qi, hp)),
            pl.BlockSpec((None, s, LANES), lambda bi, hp, qi: (bi, 0, hp)),
            pl.BlockSpec((None, None, s, LANES), lambda bi, hp, qi: (bi, hp, 0, 0)),
            pl.BlockSpec((None, nk, LANES, tk), lambda bi, hp, qi: (bi, 0, hp, 0)),
            pl.BlockSpec((None, tq, LANES), lambda bi, hp, qi: (bi, qi, hp)),
        ],
        out_specs=pl.BlockSpec((None, tq, LANES), lambda bi, hp, qi: (bi, qi, hp)),
        out_shape=jax.ShapeDtypeStruct((b, s, dh), jnp.bfloat16),
        scratch_shapes=[
            pltpu.VMEM((2, 2, tk, tq), jnp.float32),
            pltpu.VMEM((2, 2, SUBLANES, tq), jnp.float32),
            pltpu.VMEM((2, SUBLANES, tq), jnp.float32),
            pltpu.VMEM((2, HEAD_DIM + BF16_ROWS, tq), jnp.float32),
        ],
        compiler_params=pltpu.CompilerParams(
            dimension_semantics=("arbitrary", "arbitrary", "arbitrary"),
            vmem_limit_bytes=48 * 1024 * 1024),
        name="prompt_attn",
    )(qb, kb, ka, vtb, sza)


def _out_kernel(x_ref, oa_ref, oc_ref, w_ref, gate_ref, y_ref):
    dh = oa_ref.shape[-1]
    y = _dot(oa_ref[...], w_ref[:dh, :]) + _dot(oc_ref[...], w_ref[dh:, :])
    y_ref[...] = x_ref[...] + gate_ref[...] * y


def _prompt_out(l, x, oa, oc, w_out, mods, tm):
    b, s, d = x.shape
    dh = oa.shape[-1]
    row_spec = lambda width: pl.BlockSpec((None, tm, width), lambda bi, i: (bi, i, 0))
    return pl.pallas_call(
        _out_kernel,
        grid=(b, s // tm),
        in_specs=[
            row_spec(d), row_spec(dh), row_spec(dh),
            pl.BlockSpec((None, 2 * dh, d), lambda bi, i: (l, 0, 0)),
            pl.BlockSpec((None, None, 1, d), lambda bi, i: (l, bi, 0, 2)),
        ],
        out_specs=row_spec(d),
        out_shape=jax.ShapeDtypeStruct((b, s, d), jnp.float32),
        compiler_params=pltpu.CompilerParams(
            dimension_semantics=("arbitrary", "arbitrary"),
            vmem_limit_bytes=48 * 1024 * 1024),
        name="prompt_out",
    )(x, oa, oc, w_out, mods)


def _sample_in_kernel(x_ref, mod_ref, st_ref, g_ref, w_ref, wf_ref, bf_ref, qg_ref, kg_ref,
                      cw_ref, cb_ref, bd_ref,
                      q_ref, k_ref, v_ref, lf_ref, sza_ref, oc_ref, hist_ref, *, d_model):
    mod = mod_ref[...]
    shift = mod[:, :d_model]
    scale = mod[:, d_model:2 * d_model]
    q, k, v, sza, logf, bc, cu, zc = _inproj_math(
        x_ref[...], shift, scale, g_ref[...], w_ref, wf_ref, bf_ref[...],
        qg_ref[...], kg_ref[...], bd_ref[...])
    dh = q.shape[-1]
    q_ref[...] = q * (LOG2E * HEAD_DIM ** -0.5)
    k_ref[...] = k
    v_ref[...] = v
    lf_ref[...] = logf
    sza_ref[...] = sza
    st = st_ref[...]
    h0 = st[:, :dh]
    h1 = st[:, dh:]
    cw = cw_ref[...]
    conv = cb_ref[...] + h0 * cw[0:1, :] + h1 * cw[1:2, :] + cu * cw[2:3, :]
    oc_ref[...] = _bf(bc * conv * _silu(zc))
    hist_ref[...] = jnp.concatenate([h1, cu], axis=1)


def _sample_in(l, x, mod_s, st, norm_g, w_main, w_f, b_f, qg, kg, conv_w, conv_b, bd):
    r, d = x.shape
    dh = qg.shape[-1]
    full = lambda shape: pl.BlockSpec(shape, lambda i: (0,) * len(shape))
    par = lambda shape: pl.BlockSpec((None,) + shape, lambda i: (l,) + (0,) * len(shape))
    kernel = functools.partial(_sample_in_kernel, d_model=d)
    f32 = jnp.float32
    return pl.pallas_call(
        kernel,
        grid=(1,),
        in_specs=[
            full((r, d)), par((r, 3 * d)), par((r, 2 * dh)), par((1, d)),
            par((d, 8 * dh)), par((d, LANES)), par((1, LANES)), par((1, dh)), par((1, dh)),
            par((CONV_W, dh)), par((1, dh)), full(bd.shape),
        ],
        out_specs=[
            full((r, dh)), full((r, dh)), full((r, dh)), full((r, LANES)), full((r, dh)),
            full((r, dh)), full((r, 2 * dh)),
        ],
        out_shape=[
            jax.ShapeDtypeStruct((r, dh), f32),
            jax.ShapeDtypeStruct((r, dh), f32),
            jax.ShapeDtypeStruct((r, dh), f32),
            jax.ShapeDtypeStruct((r, LANES), f32),
            jax.ShapeDtypeStruct((r, dh), f32),
            jax.ShapeDtypeStruct((r, dh), jnp.bfloat16),
            jax.ShapeDtypeStruct((r, 2 * dh), f32),
        ],
        compiler_params=pltpu.CompilerParams(vmem_limit_bytes=56 * 1024 * 1024),
        name="sample_in",
    )(x, mod_s, st, norm_g, w_main, w_f, b_f, qg, kg, conv_w, conv_b, bd)


def _sample_attn_kernel(pt_ref, q_ref, kn_ref, vn_ref, lfn_ref, k_hbm, v_hbm, lf_hbm,
                        o_ref, kbuf, vbuf, lfbuf, sems, acc_ref, m_ref, l_ref, c_ref,
                        *, layer_off, n_pages, cpp):
    n_heads, hd, n_tok = kbuf.shape[1], kbuf.shape[2], kbuf.shape[3]
    dh = n_heads * hd
    page = n_tok // cpp
    nch = n_pages // cpp
    g = pl.program_id(0)
    nsteps = pl.num_programs(0)
    slot = lax.rem(g, 2)
    seq = g // nch
    ch = lax.rem(g, nch)

    def copies(step, sl):
        sq = step // nch
        c0 = lax.rem(step, nch) * cpp
        out = []
        for p in range(cpp):
            pid = pt_ref[sq * n_pages + c0 + p] + layer_off
            lanes = pl.ds(p * page, page)
            out.append(pltpu.make_async_copy(k_hbm.at[pid], kbuf.at[sl, :, :, lanes], sems.at[sl, 0]))
            out.append(pltpu.make_async_copy(v_hbm.at[pid], vbuf.at[sl, :, :, lanes], sems.at[sl, 1]))
            out.append(pltpu.make_async_copy(lf_hbm.at[pid], lfbuf.at[sl, :, lanes], sems.at[sl, 2]))
        return out

    @pl.when(g == 0)
    def _():
        o_ref[...] = jnp.zeros_like(o_ref)
        for cp in copies(0, 0):
            cp.start()

    @pl.when(g + 1 < nsteps)
    def _():
        for cp in copies(g + 1, 1 - slot):
            cp.start()

    for cp in copies(g, slot):
        cp.wait()

    @pl.when(ch == 0)
    def _():
        acc_ref[...] = jnp.zeros_like(acc_ref)
        m_ref[...] = jnp.full_like(m_ref, NEG_INF)
        l_ref[...] = jnp.zeros_like(l_ref)
        c_ref[...] = jnp.zeros_like(c_ref)

    qrow = q_ref[pl.ds(seq, 1), :]
    sub = lax.broadcasted_iota(jnp.int32, (BF16_ROWS, dh), 0)
    lane_head = lax.broadcasted_iota(jnp.int32, (BF16_ROWS, dh), 1) // hd
    diag = sub == lane_head
    qbd = jnp.where(diag, qrow, 0.0)

    kc = _bf(kbuf[slot].reshape(dh, n_tok))
    s = _dot(_bf(qbd), kc)[:n_heads]
    cum = _prefix_sum_lanes(lfbuf[slot] * LOG2E) + c_ref[...]
    c_ref[...] = cum[:, n_tok - 1:n_tok]
    logit = s - cum
    m_old = m_ref[...]
    m_new = jnp.maximum(m_old, jnp.max(logit, axis=1, keepdims=True))
    alpha = jnp.exp2(m_old - m_new)
    p_t = jnp.exp2(logit - m_new)
    l_ref[...] = alpha * l_ref[...] + jnp.sum(p_t, axis=1, keepdims=True)
    m_ref[...] = m_new
    p16 = _bf(jnp.concatenate([p_t, jnp.zeros_like(p_t)], axis=0))
    vc = _bf(vbuf[slot].reshape(dh, n_tok))
    acc_ref[...] = alpha * acc_ref[...] + _dot_nt(p16, vc)[:n_heads]

    @pl.when(ch == nch - 1)
    def _():
        krow = kn_ref[pl.ds(seq, 1), :]
        vrow = vn_ref[pl.ds(seq, 1), :]
        lfrow = lfn_ref[pl.ds(seq, 1), :]
        sub8 = lax.broadcasted_iota(jnp.int32, (n_heads, LANES), 0)
        lane8 = lax.broadcasted_iota(jnp.int32, (n_heads, LANES), 1)
        lfcol = jnp.sum(jnp.where(sub8 == lane8, lfrow, 0.0), axis=1, keepdims=True) * LOG2E
        s_self = jnp.sum(qbd[:n_heads] * krow, axis=1, keepdims=True) - (c_ref[...] + lfcol)
        m_prev = m_ref[...]
        m_fin = jnp.maximum(m_prev, s_self)
        a_fin = jnp.exp2(m_prev - m_fin)
        p_self = jnp.exp2(s_self - m_fin)
        l_fin = a_fin * l_ref[...] + p_self
        o = (a_fin * acc_ref[...] + p_self * vrow) / l_fin
        o_ref[pl.ds(seq, 1), :] = jnp.sum(jnp.where(diag[:n_heads], o, 0.0), axis=0, keepdims=True)


def _sample_attn(layer_off, page_table, q, kn, vn, lfn, ck, cv, clf, n_seq):
    r, dh = q.shape
    n_heads = dh // HEAD_DIM
    n_pages = page_table.shape[1]
    cpp = min(PAGES_PER_CHUNK, n_pages)
    nch = n_pages // cpp
    page = ck.shape[-1]
    kernel = functools.partial(
        _sample_attn_kernel, layer_off=layer_off, n_pages=n_pages, cpp=cpp)
    full = lambda shape: pl.BlockSpec(shape, lambda g, pt: (0,) * len(shape))
    anyspec = pl.BlockSpec(memory_space=pl.ANY)
    grid_spec = pltpu.PrefetchScalarGridSpec(
        num_scalar_prefetch=1,
        grid=(n_seq * nch,),
        in_specs=[full((r, dh)), full((r, dh)), full((r, dh)), full((r, LANES)),
                  anyspec, anyspec, anyspec],
        out_specs=full((r, dh)),
        scratch_shapes=[
            pltpu.VMEM((2, n_heads, HEAD_DIM, cpp * page), jnp.float32),
            pltpu.VMEM((2, n_heads, HEAD_DIM, cpp * page), jnp.float32),
            pltpu.VMEM((2, n_heads, cpp * page), jnp.float32),
            pltpu.SemaphoreType.DMA((2, 3)),
            pltpu.VMEM((n_heads, dh), jnp.float32),
            pltpu.VMEM((n_heads, 1), jnp.float32),
            pltpu.VMEM((n_heads, 1), jnp.float32),
            pltpu.VMEM((n_heads, 1), jnp.float32),
        ],
    )
    return pl.pallas_call(
        kernel,
        grid_spec=grid_spec,
        out_shape=jax.ShapeDtypeStruct((r, dh), jnp.float32),
        compiler_params=pltpu.CompilerParams(
            dimension_semantics=("arbitrary",),
            vmem_limit_bytes=48 * 1024 * 1024),
        name="sample_attn",
    )(page_table.reshape(-1), q, kn, vn, lfn, ck, cv, clf)


def _sample_out_kernel(x_ref, o_ref, sza_ref, oc_ref, w_ref, mod_ref, y_ref, *, d_model):
    dh = oc_ref.shape[-1]
    oa = _bf(o_ref[...] * sza_ref[...])
    y = _dot(oa, w_ref[:dh, :]) + _dot(oc_ref[...], w_ref[dh:, :])
    gate = mod_ref[:, 2 * d_model:]
    y_ref[...] = x_ref[...] + gate * y


def _sample_out(l, x, o, sza, oc, w_out, mod_s):
    r, d = x.shape
    dh = oc.shape[-1]
    full = lambda shape: pl.BlockSpec(shape, lambda i: (0,) * len(shape))
    par = lambda shape: pl.BlockSpec((None,) + shape, lambda i: (l,) + (0,) * len(shape))
    return pl.pallas_call(
        functools.partial(_sample_out_kernel, d_model=d),
        grid=(1,),
        in_specs=[full((r, d)), full((r, dh)), full((r, dh)), full((r, dh)),
                  par((2 * dh, d)), par((r, 3 * d))],
        out_specs=full((r, d)),
        out_shape=jax.ShapeDtypeStruct((r, d), jnp.float32),
        name="sample_out",
    )(x, o, sza, oc, w_out, mod_s)


def kernel(x_prompt, x_sample, cache_k, cache_v, cache_logf, state_conv, page_table, c_prompt, c_sample,
           w_in, b_forget, w_out, norm_g, q_gain, k_gain, conv_w, conv_b, w_ada, b_ada):
    depth, d, _ = w_in.shape
    b, s, _ = x_prompt.shape
    n_seq = x_sample.shape[0]
    n_heads = b_forget.shape[-1]
    dh = n_heads * HEAD_DIM
    n_phys, page = cache_k.shape[1], cache_k.shape[2]
    tm = min(512, s)
    tq = tk = min(512, s)
    r_pad = -(-n_seq // SUBLANES) * SUBLANES

    f0 = 4 * dh
    w_main = _bf(jnp.concatenate([w_in[:, :, :f0], w_in[:, :, f0 + n_heads:]], axis=-1))
    w_f = _bf(jnp.pad(w_in[:, :, f0:f0 + n_heads], ((0, 0), (0, 0), (0, LANES - n_heads))))
    b_f = jnp.pad(b_forget, ((0, 0), (0, LANES - n_heads))).reshape(depth, 1, LANES)
    w_out_b = _bf(w_out)
    qg = jnp.tile(q_gain, (1, n_heads)).reshape(depth, 1, dh)
    kg = jnp.tile(k_gain, (1, n_heads)).reshape(depth, 1, dh)
    g3 = norm_g.reshape(depth, 1, d)
    cb3 = conv_b.reshape(depth, 1, dh)
    half = dh // 2
    hid = jnp.arange(half) // HEAD_DIM
    bd = _bf(jnp.where(hid[:, None] == hid[None, :], 1.0 / HEAD_DIM, 0.0))

    n_prompt = c_prompt.shape[0]
    c_all = jnp.concatenate([c_prompt, c_sample], axis=0)
    rows = c_all.shape[0]
    rows_pad = -(-rows // SUBLANES) * SUBLANES
    c_all = jnp.pad(c_all, ((0, rows_pad - rows), (0, 0)))
    mods = _ada(c_all, w_ada, b_ada)
    mods4 = mods.reshape(depth, rows_pad, 1, 3 * d)
    mod_s = jnp.pad(mods[:, n_prompt:n_prompt + n_seq, :], ((0, 0), (0, r_pad - n_seq), (0, 0)))

    ck = jnp.transpose(cache_k, (0, 1, 3, 4, 2)).reshape(depth * n_phys, n_heads, HEAD_DIM, page)
    cv = jnp.transpose(cache_v, (0, 1, 3, 4, 2)).reshape(depth * n_phys, n_heads, HEAD_DIM, page)
    clf = jnp.transpose(cache_logf, (0, 1, 3, 2)).reshape(depth * n_phys, n_heads, page)
    st_all = jnp.pad(state_conv.reshape(depth, n_seq, (CONV_W - 1) * dh),
                     ((0, 0), (0, r_pad - n_seq), (0, 0)))

    yp = x_prompt
    kp, vp, lp, cp = [], [], [], []
    for l in range(depth):
        qb, kb, vtb, kt, vt, lft, sza, oc, hist = _prompt_in(
            l, yp, mods4, g3, w_main, w_f, b_f, qg, kg, conv_w, cb3, bd, tm, tk)
        parts = jnp.stack(_cumsum(lft), axis=-1)
        parts = parts.reshape(b, n_heads // 2, 2, s, 3)
        ka = jnp.transpose(parts, (0, 1, 3, 2, 4)).reshape(b, n_heads // 2, s, 6)
        ka = _bf(jnp.pad(ka, ((0, 0), (0, 0), (0, 0), (0, LANES - 6))))
        oa = _attn(qb, kb, ka, vtb, sza, tq, tk)
        yp = _prompt_out(l, yp, oa, oc, w_out_b, mods4, min(1024, s))
        kp.append(jnp.transpose(kt.reshape(b, n_heads, HEAD_DIM, s), (0, 3, 1, 2)))
        vp.append(jnp.transpose(vt.reshape(b, n_heads, HEAD_DIM, s), (0, 3, 1, 2)))
        lp.append(jnp.transpose(lft, (0, 2, 1)))
        cp.append(hist)

    ys = jnp.pad(x_sample.reshape(n_seq, d), ((0, r_pad - n_seq), (0, 0)))
    ksm, vsm, lsm, csm = [], [], [], []
    for l in range(depth):
        q, k, v, lf, sza, oc, hist = _sample_in(
            l, ys, mod_s, st_all, g3, w_main, w_f, b_f, qg, kg, conv_w, cb3, bd)
        o = _sample_attn(l * n_phys, page_table, q, k, v, lf, ck, cv, clf, n_seq)
        ys = _sample_out(l, ys, o, sza, oc, w_out_b, mod_s)
        ksm.append(k[:n_seq].reshape(n_seq, 1, n_heads, HEAD_DIM))
        vsm.append(v[:n_seq].reshape(n_seq, 1, n_heads, HEAD_DIM))
        lsm.append(lf[:n_seq, :n_heads].reshape(n_seq, 1, n_heads))
        csm.append(hist[:n_seq].reshape(n_seq, CONV_W - 1, dh))

    return (yp, ys[:n_seq].reshape(n_seq, 1, d), jnp.stack(kp), jnp.stack(vp), jnp.stack(lp),
            jnp.stack(cp), jnp.stack(ksm), jnp.stack(vsm), jnp.stack(lsm), jnp.stack(csm))
```

```python
import functools

import jax
import jax.numpy as jnp
from jax import lax
from jax.experimental import pallas as pl
from jax.experimental.pallas import tpu as pltpu

HEAD_DIM = 64
CONV_W = 3
RMS_EPS = 1e-6
LANES = 128
SUBLANES = 8
BF16_ROWS = 16
LOG2E = 1.4426950408889634
NEG_INF = float("-inf")
PAGES_PER_CHUNK = 16
EXP_ROWS = 128

_NT = (((1,), (1,)), ((), ()))


def _bf(x):
    return x.astype(jnp.bfloat16)


def _dot(a, b):
    return jnp.dot(a, b, preferred_element_type=jnp.float32)


def _dot_nt(a, b):
    return lax.dot_general(a, b, _NT, preferred_element_type=jnp.float32)


def _silu(x):
    return x * (1.0 / (1.0 + jnp.exp(-x)))


def _log_sigmoid(z):
    return jnp.minimum(z, 0.0) - jnp.log(1.0 + jnp.exp(-jnp.abs(z)))


def _prefix_sum_lanes(x):
    n = x.shape[-1]
    lane = lax.broadcasted_iota(jnp.int32, x.shape, x.ndim - 1)
    sh = 1
    while sh < n:
        x = x + jnp.where(lane >= sh, pltpu.roll(x, sh, x.ndim - 1), 0.0)
        sh *= 2
    return x


def _ada_kernel(c_ref, w_ref, b_ref, o_ref):
    c = c_ref[...]
    o_ref[...] = _dot(_bf(_silu(c)), _bf(w_ref[...])) + b_ref[...]


def _ada(c_all, w_ada, b_ada):
    depth, d, n3 = w_ada.shape
    rows = c_all.shape[0]
    bn = 768
    return pl.pallas_call(
        _ada_kernel,
        grid=(depth, n3 // bn),
        in_specs=[
            pl.BlockSpec((rows, d), lambda l, j: (0, 0)),
            pl.BlockSpec((None, d, bn), lambda l, j: (l, 0, j)),
            pl.BlockSpec((None, 1, bn), lambda l, j: (l, 0, j)),
        ],
        out_specs=pl.BlockSpec((None, rows, bn), lambda l, j: (l, 0, j)),
        out_shape=jax.ShapeDtypeStruct((depth, rows, n3), jnp.float32),
        name="ada_mod",
    )(c_all, w_ada, b_ada.reshape(depth, 1, n3))


def _inproj_math(x, shift, scale, g, w_ref, wf_ref, bf_row, qg, kg, bd):
    dh = qg.shape[-1]
    ms = jnp.mean(x * x, axis=-1, keepdims=True)
    xn = x * lax.rsqrt(ms + RMS_EPS)
    h = _bf(xn * g * (1.0 + scale) + shift)

    def col(c):
        return _dot(h, w_ref[:, c * dh:(c + 1) * dh])

    def head_norm(t, gain):
        t2 = _bf(t * t)
        half = dh // 2
        msq = jnp.concatenate(
            [_dot(t2[:, :half], bd), _dot(t2[:, half:], bd)], axis=1)
        return t * lax.rsqrt(msq + RMS_EPS) * gain

    q = head_norm(col(0), qg)
    k = head_norm(col(1), kg)
    v = col(2)
    sza = _silu(col(3))
    fl = _dot(h, wf_ref[...]) + bf_row
    logf = _log_sigmoid(fl)
    bc = col(4)
    cu = col(5) * col(6)
    zc = col(7)
    return q, k, v, sza, logf, bc, cu, zc


def _prompt_in_kernel(*refs, d_model, n_heads, tk, n_alias):
    (x_ref, mod_ref, g_ref, w_ref, wf_ref, bf_ref, qg_ref, kg_ref,
     cw_ref, cb_ref, bd_ref) = refs[:11]
    (qb_ref, kb_ref, vtb_ref, kt_ref, vt_ref, lft_ref, sza_ref, oc_ref,
     hist_ref, carry_ref) = refs[11 + n_alias:]
    i = pl.program_id(1)
    tm = x_ref.shape[0]
    mod = mod_ref[...]
    shift = mod[:, :d_model]
    scale = mod[:, d_model:2 * d_model]
    q, k, v, sza, logf, bc, cu, zc = _inproj_math(
        x_ref[...], shift, scale, g_ref[...], w_ref, wf_ref, bf_ref[...],
        qg_ref[...], kg_ref[...], bd_ref[...])

    qb_ref[...] = _bf(q * (LOG2E * HEAD_DIM ** -0.5))
    kb_ref[...] = _bf(k)
    k_t = k.T
    v_t = v.T
    if n_alias:
        kt_ref[...] = k_t
        vt_ref[...] = v_t
    else:
        for slot in range(kt_ref.shape[0]):
            kt_ref[slot] = k_t
            vt_ref[slot] = v_t
    for c in range(tm // tk):
        vtb_ref[c] = _bf(v_t[:, c * tk:(c + 1) * tk])
    sza_ref[...] = sza
    lft_ref[...] = logf.T[:n_heads, :]

    @pl.when(i == 0)
    def _():
        carry_ref[...] = jnp.zeros_like(carry_ref)

    prev = carry_ref[...]
    row = lax.broadcasted_iota(jnp.int32, cu.shape, 0)
    s1 = pltpu.roll(cu, 1, 0)
    s1 = jnp.where(row == 0, prev[7:8, :], s1)
    s2 = pltpu.roll(cu, 2, 0)
    s2 = jnp.where(row == 0, prev[6:7, :], jnp.where(row == 1, prev[7:8, :], s2))
    cw = cw_ref[...]
    conv = cb_ref[...] + s2 * cw[0:1, :] + s1 * cw[1:2, :] + cu * cw[2:3, :]
    oc_ref[...] = _bf(bc * conv * _silu(zc))
    carry_ref[...] = cu[tm - SUBLANES:, :]

    @pl.when(i == pl.num_programs(1) - 1)
    def _():
        hist_ref[...] = cu[tm - (CONV_W - 1):, :]


def _prompt_in(l, x, mods, norm_g, w_main, w_f, b_f, qg, kg, conv_w, conv_b, bd, tm, tk, kv_prev):
    depth = w_main.shape[0]
    b, s, d = x.shape
    dh = qg.shape[-1]
    n_heads = dh // HEAD_DIM
    nk = s // tk
    grid = (b, s // tm)
    row_spec = lambda width: pl.BlockSpec((None, tm, width), lambda bi, i: (bi, i, 0))
    col_spec = lambda height: pl.BlockSpec((None, height, tm), lambda bi, i: (bi, 0, i))
    if kv_prev:
        stacked_spec = pl.BlockSpec((None, None, dh, tm), lambda bi, i: (l, bi, 0, i))
    else:
        stacked_spec = pl.BlockSpec((depth, None, dh, tm), lambda bi, i: (0, bi, 0, i))
    par = lambda shape: pl.BlockSpec((None,) + shape, lambda bi, i: (l,) + (0,) * len(shape))
    n_alias = len(kv_prev)
    n_in = 11
    kernel = functools.partial(_prompt_in_kernel, d_model=d, n_heads=n_heads, tk=tk, n_alias=n_alias)
    return pl.pallas_call(
        kernel,
        grid=grid,
        input_output_aliases={n_in + a: 3 + a for a in range(n_alias)},
        in_specs=[
            row_spec(d),
            pl.BlockSpec((None, None, 1, 3 * d), lambda bi, i: (l, bi, 0, 0)),
            par((1, d)),
            pl.BlockSpec((None, d, 8 * dh), lambda bi, i: (l, 0, 0),
                         pipeline_mode=pl.Buffered(1)),
            par((d, LANES)),
            par((1, LANES)),
            par((1, dh)),
            par((1, dh)),
            par((CONV_W, dh)),
            par((1, dh)),
            pl.BlockSpec(bd.shape, lambda bi, i: (0, 0)),
        ] + [pl.BlockSpec(memory_space=pl.ANY)] * n_alias,
        out_specs=[
            row_spec(dh),
            row_spec(dh),
            pl.BlockSpec((None, tm // tk, dh, tk), lambda bi, i: (bi, i, 0, 0)),
            stacked_spec,
            stacked_spec,
            col_spec(n_heads),
            row_spec(dh),
            row_spec(dh),
            pl.BlockSpec((None, CONV_W - 1, dh), lambda bi, i: (bi, 0, 0)),
        ],
        out_shape=[
            jax.ShapeDtypeStruct((b, s, dh), jnp.bfloat16),
            jax.ShapeDtypeStruct((b, s, dh), jnp.bfloat16),
            jax.ShapeDtypeStruct((b, nk, dh, tk), jnp.bfloat16),
            jax.ShapeDtypeStruct((depth, b, dh, s), jnp.float32),
            jax.ShapeDtypeStruct((depth, b, dh, s), jnp.float32),
            jax.ShapeDtypeStruct((b, n_heads, s), jnp.float32),
            jax.ShapeDtypeStruct((b, s, dh), jnp.float32),
            jax.ShapeDtypeStruct((b, s, dh), jnp.bfloat16),
            jax.ShapeDtypeStruct((b, CONV_W - 1, dh), jnp.float32),
        ],
        scratch_shapes=[pltpu.VMEM((SUBLANES, dh), jnp.float32)],
        compiler_params=pltpu.CompilerParams(
            dimension_semantics=("arbitrary", "arbitrary"),
            vmem_limit_bytes=56 * 1024 * 1024),
        name="prompt_in",
    )(x, mods, norm_g, w_main, w_f, b_f, qg, kg, conv_w, conv_b, bd, *kv_prev)


def _cumsum_kernel(lf_ref, hi_ref, mid_ref, lo_ref):
    x = -_prefix_sum_lanes(lf_ref[...] * LOG2E)
    hi = _bf(x).astype(jnp.float32)
    r = x - hi
    mid = _bf(r).astype(jnp.float32)
    hi_ref[...] = hi
    mid_ref[...] = mid
    lo_ref[...] = r - mid


def _cumsum(lf_t):
    b, h, s = lf_t.shape
    spec = pl.BlockSpec((None, h, s), lambda bi: (bi, 0, 0))
    shape = jax.ShapeDtypeStruct((b, h, s), jnp.float32)
    return pl.pallas_call(
        _cumsum_kernel,
        grid=(b,),
        in_specs=[spec],
        out_specs=[spec, spec, spec],
        out_shape=[shape, shape, shape],
        name="forget_cumsum",
    )(lf_t)


def _attn_q_block(qi, q_ref, k_ref, ka_ref, vt_ref, sza_ref, o_ref,
                  s_scr, tmax_scr, m_scr, acc_scr, *, tq, tk):
    q = q_ref[...]
    lane = lax.broadcasted_iota(jnp.int32, q.shape, 1)
    zero = jnp.zeros_like(q)
    q_split = (jnp.where(lane < HEAD_DIM, q, zero), jnp.where(lane >= HEAD_DIM, q, zero))
    q_heads = []
    for hh in range(2):
        bias_ones = _bf(jnp.where((lane >= 3 * hh) & (lane < 3 * hh + 3), 1.0, 0.0))
        q_heads.append(jnp.concatenate([q_split[hh], bias_ones], axis=1))
    ones_rows = jnp.ones((BF16_ROWS, tk), jnp.bfloat16)

    m_scr[...] = jnp.full_like(m_scr, NEG_INF)
    acc_scr[...] = jnp.zeros_like(acc_scr)

    def qk(j):
        rows = pl.ds(pl.multiple_of(j * tk, tk), tk)
        k_t = jnp.concatenate([k_ref[rows, :], ka_ref[rows, :]], axis=1)
        return [_dot_nt(k_t, q_heads[hh]) for hh in range(2)]

    def scores_to_scratch(s_pair, masked, slot):
        for hh in range(2):
            s_t = s_pair[hh]
            if masked:
                kpos = lax.broadcasted_iota(jnp.int32, s_t.shape, 0)
                qpos = lax.broadcasted_iota(jnp.int32, s_t.shape, 1)
                s_t = jnp.where(kpos <= qpos, s_t, NEG_INF)
            s_scr[slot, hh] = s_t
            col_max = jnp.max(s_t, axis=0, keepdims=True)
            tmax_scr[slot, hh] = jnp.broadcast_to(col_max, (SUBLANES, tq))

    def rep(stat, rows):
        return jnp.broadcast_to(stat[None], (rows // SUBLANES, SUBLANES, tq)).reshape(rows, tq)

    def exps(slot):
        out = []
        for hh in range(2):
            m_old = m_scr[hh]
            m_new = jnp.maximum(m_old, tmax_scr[slot, hh])
            alpha = jnp.exp2(m_old - m_new)
            m_scr[hh] = m_new
            m_rep = rep(m_new, EXP_ROWS)
            p = jnp.concatenate(
                [_bf(jnp.exp2(s_scr[slot, hh, c * EXP_ROWS:(c + 1) * EXP_ROWS, :] - m_rep))
                 for c in range(tk // EXP_ROWS)], axis=0)
            out.append((alpha, p))
        return out

    def pv(j, ap):
        out = []
        for hh in range(2):
            alpha, p = ap[hh]
            v_t = jnp.concatenate(
                [vt_ref[j, hh * HEAD_DIM:(hh + 1) * HEAD_DIM, :], ones_rows], axis=0)
            out.append((alpha, _dot(v_t, p)))
        return out

    def accumulate(av):
        for hh in range(2):
            alpha, pv_t = av[hh]
            acc_scr[hh] = rep(alpha, pv_t.shape[0]) * acc_scr[hh] + pv_t

    def step(j_next, masked, slot_next):
        ap = exps(1 - slot_next)
        s_next = qk(j_next)
        av = pv(j_next - 1, ap)
        scores_to_scratch(s_next, masked, slot_next)
        accumulate(av)

    def finalize(slot):
        accumulate(pv(qi, exps(slot)))
        o_t = jnp.concatenate(
            [acc_scr[hh, :HEAD_DIM, :] / acc_scr[hh, HEAD_DIM:HEAD_DIM + 1, :] for hh in range(2)],
            axis=0)
        o_ref[...] = _bf(o_t.T * sza_ref[...])

    @pl.when(qi == 0)
    def _():
        scores_to_scratch(qk(0), True, 0)
        finalize(0)

    @pl.when(qi > 0)
    def _():
        scores_to_scratch(qk(0), False, 0)

    n_plain = jnp.maximum(qi - 1, 0)

    def body(jj, carry):
        step(2 * jj + 1, False, 1)
        step(2 * jj + 2, False, 0)
        return carry

    lax.fori_loop(0, n_plain // 2, body, 0)

    @pl.when(lax.rem(n_plain, 2) == 1)
    def _():
        step(qi - 1, False, 1)

    for parity in range(2):
        @pl.when((qi > 0) & (lax.rem(qi, 2) == parity))
        def _():
            step(qi, True, parity)
            finalize(parity)


def _attn_kernel(q_ref, k_ref, ka_ref, vt_ref, sza_ref, o_ref, *scratch, tq, tk):
    def q_loop(qi, carry):
        rows = pl.ds(pl.multiple_of(qi * tq, tq), tq)
        _attn_q_block(qi, q_ref.at[rows], k_ref, ka_ref, vt_ref, sza_ref.at[rows], o_ref.at[rows],
                      *scratch, tq=tq, tk=tk)
        return carry

    lax.fori_loop(0, q_ref.shape[0] // tq, q_loop, 0)


def _attn(qb, kb, ka, vtb, sza, tq, tk):
    b, s, dh = qb.shape
    npair = dh // LANES
    nk = s // tk
    kernel = functools.partial(_attn_kernel, tq=tq, tk=tk)
    seq_spec = pl.BlockSpec((None, s, LANES), lambda bi, hp: (bi, 0, hp))
    return pl.pallas_call(
        kernel,
        grid=(b, npair),
        in_specs=[
            seq_spec,
            seq_spec,
            pl.BlockSpec((None, None, s, LANES), lambda bi, hp: (bi, hp, 0, 0)),
            pl.BlockSpec((None, nk, LANES, tk), lambda bi, hp: (bi, 0, hp, 0)),
            seq_spec,
        ],
        out_specs=seq_spec,
        out_shape=jax.ShapeDtypeStruct((b, s, dh), jnp.bfloat16),
        scratch_shapes=[
            pltpu.VMEM((2, 2, tk, tq), jnp.float32),
            pltpu.VMEM((2, 2, SUBLANES, tq), jnp.float32),
            pltpu.VMEM((2, SUBLANES, tq), jnp.float32),
            pltpu.VMEM((2, HEAD_DIM + BF16_ROWS, tq), jnp.float32),
        ],
        compiler_params=pltpu.CompilerParams(
            dimension_semantics=("arbitrary", "arbitrary"),
            vmem_limit_bytes=48 * 1024 * 1024),
        name="prompt_attn",
    )(qb, kb, ka, vtb, sza)


def _out_kernel(x_ref, oa_ref, oc_ref, w_ref, gate_ref, y_ref):
    dh = oa_ref.shape[-1]
    y = _dot(oa_ref[...], w_ref[:dh, :]) + _dot(oc_ref[...], w_ref[dh:, :])
    y_ref[...] = x_ref[...] + gate_ref[...] * y


def _prompt_out(l, x, oa, oc, w_out, mods, tm):
    b, s, d = x.shape
    dh = oa.shape[-1]
    row_spec = lambda width: pl.BlockSpec((None, tm, width), lambda bi, i: (bi, i, 0))
    return pl.pallas_call(
        _out_kernel,
        grid=(b, s // tm),
        in_specs=[
            row_spec(d), row_spec(dh), row_spec(dh),
            pl.BlockSpec((None, 2 * dh, d), lambda bi, i: (l, 0, 0)),
            pl.BlockSpec((None, None, 1, d), lambda bi, i: (l, bi, 0, 2)),
        ],
        out_specs=row_spec(d),
        out_shape=jax.ShapeDtypeStruct((b, s, d), jnp.float32),
        compiler_params=pltpu.CompilerParams(
            dimension_semantics=("arbitrary", "arbitrary"),
            vmem_limit_bytes=48 * 1024 * 1024),
        name="prompt_out",
    )(x, oa, oc, w_out, mods)


def _sample_in_kernel(x_ref, mod_ref, st_ref, g_ref, w_ref, wf_ref, bf_ref, qg_ref, kg_ref,
                      cw_ref, cb_ref, bd_ref,
                      q_ref, k_ref, v_ref, lf_ref, sza_ref, oc_ref, hist_ref, *, d_model):
    mod = mod_ref[...]
    shift = mod[:, :d_model]
    scale = mod[:, d_model:2 * d_model]
    q, k, v, sza, logf, bc, cu, zc = _inproj_math(
        x_ref[...], shift, scale, g_ref[...], w_ref, wf_ref, bf_ref[...],
        qg_ref[...], kg_ref[...], bd_ref[...])
    dh = q.shape[-1]
    q_ref[...] = q * (LOG2E * HEAD_DIM ** -0.5)
    k_ref[...] = k
    v_ref[...] = v
    lf_ref[...] = logf
    sza_ref[...] = sza
    st = st_ref[...]
    h0 = st[:, :dh]
    h1 = st[:, dh:]
    cw = cw_ref[...]
    conv = cb_ref[...] + h0 * cw[0:1, :] + h1 * cw[1:2, :] + cu * cw[2:3, :]
    oc_ref[...] = _bf(bc * conv * _silu(zc))
    hist_ref[...] = jnp.concatenate([h1, cu], axis=1)


def _sample_in(l, x, mod_s, st, norm_g, w_main, w_f, b_f, qg, kg, conv_w, conv_b, bd):
    r, d = x.shape
    dh = qg.shape[-1]
    full = lambda shape: pl.BlockSpec(shape, lambda i: (0,) * len(shape))
    par = lambda shape: pl.BlockSpec((None,) + shape, lambda i: (l,) + (0,) * len(shape))
    kernel = functools.partial(_sample_in_kernel, d_model=d)
    f32 = jnp.float32
    return pl.pallas_call(
        kernel,
        grid=(1,),
        in_specs=[
            full((r, d)), par((r, 3 * d)), par((r, 2 * dh)), par((1, d)),
            par((d, 8 * dh)), par((d, LANES)), par((1, LANES)), par((1, dh)), par((1, dh)),
            par((CONV_W, dh)), par((1, dh)), full(bd.shape),
        ],
        out_specs=[
            full((r, dh)), full((r, dh)), full((r, dh)), full((r, LANES)), full((r, dh)),
            full((r, dh)), full((r, 2 * dh)),
        ],
        out_shape=[
            jax.ShapeDtypeStruct((r, dh), f32),
            jax.ShapeDtypeStruct((r, dh), f32),
            jax.ShapeDtypeStruct((r, dh), f32),
            jax.ShapeDtypeStruct((r, LANES), f32),
            jax.ShapeDtypeStruct((r, dh), f32),
            jax.ShapeDtypeStruct((r, dh), jnp.bfloat16),
            jax.ShapeDtypeStruct((r, 2 * dh), f32),
        ],
        compiler_params=pltpu.CompilerParams(vmem_limit_bytes=56 * 1024 * 1024),
        name="sample_in",
    )(x, mod_s, st, norm_g, w_main, w_f, b_f, qg, kg, conv_w, conv_b, bd)


def _sample_attn_kernel(pt_ref, q_ref, kn_ref, vn_ref, lfn_ref, k_hbm, v_hbm, lf_hbm,
                        o_ref, kbuf, vbuf, lfbuf, sems, acc_ref, m_ref, l_ref, c_ref,
                        *, layer_off, n_pages, cpp):
    n_heads, hd, n_tok = kbuf.shape[1], kbuf.shape[2], kbuf.shape[3]
    dh = n_heads * hd
    page = n_tok // cpp
    nch = n_pages // cpp
    g = pl.program_id(0)
    nsteps = pl.num_programs(0)
    slot = lax.rem(g, 2)
    seq = g // nch
    ch = lax.rem(g, nch)

    def copies(step, sl):
        sq = step // nch
        c0 = lax.rem(step, nch) * cpp
        out = []
        for p in range(cpp):
            pid = pt_ref[sq * n_pages + c0 + p] + layer_off
            lanes = pl.ds(p * page, page)
            out.append(pltpu.make_async_copy(k_hbm.at[pid], kbuf.at[sl, :, :, lanes], sems.at[sl, 0]))
            out.append(pltpu.make_async_copy(v_hbm.at[pid], vbuf.at[sl, :, :, lanes], sems.at[sl, 1]))
            out.append(pltpu.make_async_copy(lf_hbm.at[pid], lfbuf.at[sl, :, lanes], sems.at[sl, 2]))
        return out

    @pl.when(g == 0)
    def _():
        o_ref[...] = jnp.zeros_like(o_ref)
        for cp in copies(0, 0):
            cp.start()

    @pl.when(g + 1 < nsteps)
    def _():
        for cp in copies(g + 1, 1 - slot):
            cp.start()

    for cp in copies(g, slot):
        cp.wait()

    @pl.when(ch == 0)
    def _():
        acc_ref[...] = jnp.zeros_like(acc_ref)
        m_ref[...] = jnp.full_like(m_ref, NEG_INF)
        l_ref[...] = jnp.zeros_like(l_ref)
        c_ref[...] = jnp.zeros_like(c_ref)

    qrow = q_ref[pl.ds(seq, 1), :]
    sub = lax.broadcasted_iota(jnp.int32, (BF16_ROWS, dh), 0)
    lane_head = lax.broadcasted_iota(jnp.int32, (BF16_ROWS, dh), 1) // hd
    diag = sub == lane_head
    qbd = jnp.where(diag, qrow, 0.0)

    kc = _bf(kbuf[slot].reshape(dh, n_tok))
    s = _dot(_bf(qbd), kc)[:n_heads]
    cum = _prefix_sum_lanes(lfbuf[slot] * LOG2E) + c_ref[...]
    c_ref[...] = cum[:, n_tok - 1:n_tok]
    logit = s - cum
    m_old = m_ref[...]
    m_new = jnp.maximum(m_old, jnp.max(logit, axis=1, keepdims=True))
    alpha = jnp.exp2(m_old - m_new)
    p_t = jnp.exp2(logit - m_new)
    l_ref[...] = alpha * l_ref[...] + jnp.sum(p_t, axis=1, keepdims=True)
    m_ref[...] = m_new
    p16 = _bf(jnp.concatenate([p_t, jnp.zeros_like(p_t)], axis=0))
    vc = _bf(vbuf[slot].reshape(dh, n_tok))
    acc_ref[...] = alpha * acc_ref[...] + _dot_nt(p16, vc)[:n_heads]

    @pl.when(ch == nch - 1)
    def _():
        krow = kn_ref[pl.ds(seq, 1), :]
        vrow = vn_ref[pl.ds(seq, 1), :]
        lfrow = lfn_ref[pl.ds(seq, 1), :]
        sub8 = lax.broadcasted_iota(jnp.int32, (n_heads, LANES), 0)
        lane8 = lax.broadcasted_iota(jnp.int32, (n_heads, LANES), 1)
        lfcol = jnp.sum(jnp.where(sub8 == lane8, lfrow, 0.0), axis=1, keepdims=True) * LOG2E
        s_self = jnp.sum(qbd[:n_heads] * krow, axis=1, keepdims=True) - (c_ref[...] + lfcol)
        m_prev = m_ref[...]
        m_fin = jnp.maximum(m_prev, s_self)
        a_fin = jnp.exp2(m_prev - m_fin)
        p_self = jnp.exp2(s_self - m_fin)
        l_fin = a_fin * l_ref[...] + p_self
        o = (a_fin * acc_ref[...] + p_self * vrow) / l_fin
        o_ref[pl.ds(seq, 1), :] = jnp.sum(jnp.where(diag[:n_heads], o, 0.0), axis=0, keepdims=True)


def _sample_attn(layer_off, page_table, q, kn, vn, lfn, ck, cv, clf, n_seq):
    r, dh = q.shape
    n_heads = dh // HEAD_DIM
    n_pages = page_table.shape[1]
    cpp = min(PAGES_PER_CHUNK, n_pages)
    nch = n_pages // cpp
    page = ck.shape[-1]
    kernel = functools.partial(
        _sample_attn_kernel, layer_off=layer_off, n_pages=n_pages, cpp=cpp)
    full = lambda shape: pl.BlockSpec(shape, lambda g, pt: (0,) * len(shape))
    anyspec = pl.BlockSpec(memory_space=pl.ANY)
    grid_spec = pltpu.PrefetchScalarGridSpec(
        num_scalar_prefetch=1,
        grid=(n_seq * nch,),
        in_specs=[full((r, dh)), full((r, dh)), full((r, dh)), full((r, LANES)),
                  anyspec, anyspec, anyspec],
        out_specs=full((r, dh)),
        scratch_shapes=[
            pltpu.VMEM((2, n_heads, HEAD_DIM, cpp * page), jnp.float32),
            pltpu.VMEM((2, n_heads, HEAD_DIM, cpp * page), jnp.float32),
            pltpu.VMEM((2, n_heads, cpp * page), jnp.float32),
            pltpu.SemaphoreType.DMA((2, 3)),
            pltpu.VMEM((n_heads, dh), jnp.float32),
            pltpu.VMEM((n_heads, 1), jnp.float32),
            pltpu.VMEM((n_heads, 1), jnp.float32),
            pltpu.VMEM((n_heads, 1), jnp.float32),
        ],
    )
    return pl.pallas_call(
        kernel,
        grid_spec=grid_spec,
        out_shape=jax.ShapeDtypeStruct((r, dh), jnp.float32),
        compiler_params=pltpu.CompilerParams(
            dimension_semantics=("arbitrary",),
            vmem_limit_bytes=48 * 1024 * 1024),
        name="sample_attn",
    )(page_table.reshape(-1), q, kn, vn, lfn, ck, cv, clf)


def _sample_out_kernel(x_ref, o_ref, sza_ref, oc_ref, w_ref, mod_ref, y_ref, *, d_model):
    dh = oc_ref.shape[-1]
    oa = _bf(o_ref[...] * sza_ref[...])
    y = _dot(oa, w_ref[:dh, :]) + _dot(oc_ref[...], w_ref[dh:, :])
    gate = mod_ref[:, 2 * d_model:]
    y_ref[...] = x_ref[...] + gate * y


def _sample_out(l, x, o, sza, oc, w_out, mod_s):
    r, d = x.shape
    dh = oc.shape[-1]
    full = lambda shape: pl.BlockSpec(shape, lambda i: (0,) * len(shape))
    par = lambda shape: pl.BlockSpec((None,) + shape, lambda i: (l,) + (0,) * len(shape))
    return pl.pallas_call(
        functools.partial(_sample_out_kernel, d_model=d),
        grid=(1,),
        in_specs=[full((r, d)), full((r, dh)), full((r, dh)), full((r, dh)),
                  par((2 * dh, d)), par((r, 3 * d))],
        out_specs=full((r, d)),
        out_shape=jax.ShapeDtypeStruct((r, d), jnp.float32),
        name="sample_out",
    )(x, o, sza, oc, w_out, mod_s)


def kernel(x_prompt, x_sample, cache_k, cache_v, cache_logf, state_conv, page_table, c_prompt, c_sample,
           w_in, b_forget, w_out, norm_g, q_gain, k_gain, conv_w, conv_b, w_ada, b_ada):
    depth, d, _ = w_in.shape
    b, s, _ = x_prompt.shape
    n_seq = x_sample.shape[0]
    n_heads = b_forget.shape[-1]
    dh = n_heads * HEAD_DIM
    n_phys, page = cache_k.shape[1], cache_k.shape[2]
    tm = min(512, s)
    tq = tk = min(512, s)
    r_pad = -(-n_seq // SUBLANES) * SUBLANES

    f0 = 4 * dh
    w_main = _bf(jnp.concatenate([w_in[:, :, :f0], w_in[:, :, f0 + n_heads:]], axis=-1))
    w_f = _bf(jnp.pad(w_in[:, :, f0:f0 + n_heads], ((0, 0), (0, 0), (0, LANES - n_heads))))
    b_f = jnp.pad(b_forget, ((0, 0), (0, LANES - n_heads))).reshape(depth, 1, LANES)
    w_out_b = _bf(w_out)
    qg = jnp.tile(q_gain, (1, n_heads)).reshape(depth, 1, dh)
    kg = jnp.tile(k_gain, (1, n_heads)).reshape(depth, 1, dh)
    g3 = norm_g.reshape(depth, 1, d)
    cb3 = conv_b.reshape(depth, 1, dh)
    half = dh // 2
    hid = jnp.arange(half) // HEAD_DIM
    bd = _bf(jnp.where(hid[:, None] == hid[None, :], 1.0 / HEAD_DIM, 0.0))

    n_prompt = c_prompt.shape[0]
    c_all = jnp.concatenate([c_prompt, c_sample], axis=0)
    rows = c_all.shape[0]
    rows_pad = -(-rows // SUBLANES) * SUBLANES
    c_all = jnp.pad(c_all, ((0, rows_pad - rows), (0, 0)))
    mods = _ada(c_all, w_ada, b_ada)
    mods4 = mods.reshape(depth, rows_pad, 1, 3 * d)
    mod_s = jnp.pad(mods[:, n_prompt:n_prompt + n_seq, :], ((0, 0), (0, r_pad - n_seq), (0, 0)))

    ck = jnp.transpose(cache_k, (0, 1, 3, 4, 2)).reshape(depth * n_phys, n_heads, HEAD_DIM, page)
    cv = jnp.transpose(cache_v, (0, 1, 3, 4, 2)).reshape(depth * n_phys, n_heads, HEAD_DIM, page)
    clf = jnp.transpose(cache_logf, (0, 1, 3, 2)).reshape(depth * n_phys, n_heads, page)
    st_all = jnp.pad(state_conv.reshape(depth, n_seq, (CONV_W - 1) * dh),
                     ((0, 0), (0, r_pad - n_seq), (0, 0)))

    yp = x_prompt
    lp, cp = [], []
    kv_all = ()
    for l in range(depth):
        qb, kb, vtb, kt_all, vt_all, lft, sza, oc, hist = _prompt_in(
            l, yp, mods4, g3, w_main, w_f, b_f, qg, kg, conv_w, cb3, bd, tm, tk, kv_all)
        kv_all = (kt_all, vt_all)
        parts = jnp.stack(_cumsum(lft), axis=-1)
        parts = parts.reshape(b, n_heads // 2, 2, s, 3)
        ka = jnp.transpose(parts, (0, 1, 3, 2, 4)).reshape(b, n_heads // 2, s, 6)
        ka = _bf(jnp.pad(ka, ((0, 0), (0, 0), (0, 0), (0, LANES - 6))))
        oa = _attn(qb, kb, ka, vtb, sza, tq, tk)
        yp = _prompt_out(l, yp, oa, oc, w_out_b, mods4, min(1024, s))
        lp.append(jnp.transpose(lft, (0, 2, 1)))
        cp.append(hist)
    k_prompt, v_prompt = (
        jnp.transpose(a.reshape(depth, b, n_heads, HEAD_DIM, s), (0, 1, 4, 2, 3)) for a in kv_all)

    ys = jnp.pad(x_sample.reshape(n_seq, d), ((0, r_pad - n_seq), (0, 0)))
    ksm, vsm, lsm, csm = [], [], [], []
    for l in range(depth):
        q, k, v, lf, sza, oc, hist = _sample_in(
            l, ys, mod_s, st_all, g3, w_main, w_f, b_f, qg, kg, conv_w, cb3, bd)
        o = _sample_attn(l * n_phys, page_table, q, k, v, lf, ck, cv, clf, n_seq)
        ys = _sample_out(l, ys, o, sza, oc, w_out_b, mod_s)
        ksm.append(k[:n_seq].reshape(n_seq, 1, n_heads, HEAD_DIM))
        vsm.append(v[:n_seq].reshape(n_seq, 1, n_heads, HEAD_DIM))
        lsm.append(lf[:n_seq, :n_heads].reshape(n_seq, 1, n_heads))
        csm.append(hist[:n_seq].reshape(n_seq, CONV_W - 1, dh))

    return (yp, ys[:n_seq].reshape(n_seq, 1, d), k_prompt, v_prompt, jnp.stack(lp),
            jnp.stack(cp), jnp.stack(ksm), jnp.stack(vsm), jnp.stack(lsm), jnp.stack(csm))
```

```python
import functools

import jax
import jax.numpy as jnp
from jax import lax
from jax.experimental import pallas as pl
from jax.experimental.pallas import tpu as pltpu

HEAD_DIM = 64
CONV_W = 3
RMS_EPS = 1e-6
LANES = 128
SUBLANES = 8
BF16_ROWS = 16
LOG2E = 1.4426950408889634
NEG_INF = float("-inf")
PAGES_PER_CHUNK = 16
EXP_ROWS = 128
N_PROMPT_IN_INPUTS = 11

_NT = (((1,), (1,)), ((), ()))


def _bf(x):
    return x.astype(jnp.bfloat16)


def _dot(a, b):
    return jnp.dot(a, b, preferred_element_type=jnp.float32)


def _dot_nt(a, b):
    return lax.dot_general(a, b, _NT, preferred_element_type=jnp.float32)


def _silu(x):
    return x * (1.0 / (1.0 + jnp.exp(-x)))


def _log_sigmoid(z):
    return jnp.minimum(z, 0.0) - jnp.log(1.0 + jnp.exp(-jnp.abs(z)))


def _prefix_sum_lanes(x):
    n = x.shape[-1]
    lane = lax.broadcasted_iota(jnp.int32, x.shape, x.ndim - 1)
    sh = 1
    while sh < n:
        x = x + jnp.where(lane >= sh, pltpu.roll(x, sh, x.ndim - 1), 0.0)
        sh *= 2
    return x


def _ada_kernel(c_ref, w_ref, b_ref, o_ref):
    c = c_ref[...]
    o_ref[...] = _dot(_bf(_silu(c)), _bf(w_ref[...])) + b_ref[...]


def _ada(c_all, w_ada, b_ada):
    depth, d, n3 = w_ada.shape
    rows = c_all.shape[0]
    bn = 768
    return pl.pallas_call(
        _ada_kernel,
        grid=(depth, n3 // bn),
        in_specs=[
            pl.BlockSpec((rows, d), lambda l, j: (0, 0)),
            pl.BlockSpec((None, d, bn), lambda l, j: (l, 0, j)),
            pl.BlockSpec((None, 1, bn), lambda l, j: (l, 0, j)),
        ],
        out_specs=pl.BlockSpec((None, rows, bn), lambda l, j: (l, 0, j)),
        out_shape=jax.ShapeDtypeStruct((depth, rows, n3), jnp.float32),
        name="ada_mod",
    )(c_all, w_ada, b_ada.reshape(depth, 1, n3))


def _repack_kernel(wt_ref, o_ref):
    o_ref[...] = _bf(wt_ref[0].T)


def _repack_w_in(w_t, dh, n_skip):
    depth, _, d = w_t.shape
    n_groups = 8

    def src_rows(l, c):
        row = c * dh + jnp.where(c >= n_groups // 2, n_skip, 0)
        return (l, pl.multiple_of(row, SUBLANES), 0)

    return pl.pallas_call(
        _repack_kernel,
        grid=(depth, n_groups),
        in_specs=[pl.BlockSpec((pl.Element(1), pl.Element(dh), pl.Element(d)), src_rows)],
        out_specs=pl.BlockSpec((None, d, dh), lambda l, c: (l, 0, c)),
        out_shape=jax.ShapeDtypeStruct((depth, d, n_groups * dh), jnp.bfloat16),
        name="repack_w_in",
    )(w_t)


def _inproj_math(x, shift, scale, g, w_ref, wf_ref, bf_row, qg, kg, bd):
    dh = qg.shape[-1]
    ms = jnp.mean(x * x, axis=-1, keepdims=True)
    xn = x * lax.rsqrt(ms + RMS_EPS)
    h = _bf(xn * g * (1.0 + scale) + shift)

    def col(c):
        return _dot(h, w_ref[:, c * dh:(c + 1) * dh])

    def head_norm(t, gain):
        t2 = _bf(t * t)
        half = dh // 2
        msq = jnp.concatenate(
            [_dot(t2[:, :half], bd), _dot(t2[:, half:], bd)], axis=1)
        return t * lax.rsqrt(msq + RMS_EPS) * gain

    q = head_norm(col(0), qg)
    k = head_norm(col(1), kg)
    v = col(2)
    sza = _silu(col(3))
    fl = _dot(h, wf_ref[...]) + bf_row
    logf = _log_sigmoid(fl)
    bc = col(4)
    cu = col(5) * col(6)
    zc = col(7)
    return q, k, v, sza, logf, bc, cu, zc


def _prompt_in_kernel(*refs, d_model, n_heads, tk, n_alias):
    (x_ref, mod_ref, g_ref, w_ref, wf_ref, bf_ref, qg_ref, kg_ref,
     cw_ref, cb_ref, bd_ref) = refs[:N_PROMPT_IN_INPUTS]
    (qb_ref, kb_ref, vtb_ref, kt_ref, vt_ref, lft_ref, sza_ref, oc_ref,
     hist_ref, carry_ref) = refs[N_PROMPT_IN_INPUTS + n_alias:]
    i = pl.program_id(1)
    tm = x_ref.shape[0]
    mod = mod_ref[...]
    shift = mod[:, :d_model]
    scale = mod[:, d_model:2 * d_model]
    q, k, v, sza, logf, bc, cu, zc = _inproj_math(
        x_ref[...], shift, scale, g_ref[...], w_ref, wf_ref, bf_ref[...],
        qg_ref[...], kg_ref[...], bd_ref[...])

    qb_ref[...] = _bf(q * (LOG2E * HEAD_DIM ** -0.5))
    kb_ref[...] = _bf(k)
    k_t = k.T
    v_t = v.T
    if n_alias:
        kt_ref[...] = k_t
        vt_ref[...] = v_t
    else:
        for slot in range(kt_ref.shape[0]):
            kt_ref[slot] = k_t
            vt_ref[slot] = v_t
    for c in range(tm // tk):
        vtb_ref[c] = _bf(v_t[:, c * tk:(c + 1) * tk])
    sza_ref[...] = sza
    lft_ref[...] = logf.T[:n_heads, :]

    @pl.when(i == 0)
    def _():
        carry_ref[...] = jnp.zeros_like(carry_ref)

    prev = carry_ref[...]
    row = lax.broadcasted_iota(jnp.int32, cu.shape, 0)
    s1 = pltpu.roll(cu, 1, 0)
    s1 = jnp.where(row == 0, prev[7:8, :], s1)
    s2 = pltpu.roll(cu, 2, 0)
    s2 = jnp.where(row == 0, prev[6:7, :], jnp.where(row == 1, prev[7:8, :], s2))
    cw = cw_ref[...]
    conv = cb_ref[...] + s2 * cw[0:1, :] + s1 * cw[1:2, :] + cu * cw[2:3, :]
    oc_ref[...] = _bf(bc * conv * _silu(zc))
    carry_ref[...] = cu[tm - SUBLANES:, :]

    @pl.when(i == pl.num_programs(1) - 1)
    def _():
        hist_ref[...] = cu[tm - (CONV_W - 1):, :]


def _prompt_in(l, x, mods, norm_g, w_main, w_f, b_f, qg, kg, conv_w, conv_b, bd, tm, tk, kv_prev):
    depth = w_main.shape[0]
    b, s, d = x.shape
    dh = qg.shape[-1]
    n_heads = dh // HEAD_DIM
    nk = s // tk
    grid = (b, s // tm)
    row_spec = lambda width: pl.BlockSpec((None, tm, width), lambda bi, i: (bi, i, 0))
    col_spec = lambda height: pl.BlockSpec((None, height, tm), lambda bi, i: (bi, 0, i))
    if kv_prev:
        stacked_spec = pl.BlockSpec((None, None, dh, tm), lambda bi, i: (l, bi, 0, i))
    else:
        stacked_spec = pl.BlockSpec((depth, None, dh, tm), lambda bi, i: (0, bi, 0, i))
    par = lambda shape: pl.BlockSpec((None,) + shape, lambda bi, i: (l,) + (0,) * len(shape))
    n_alias = len(kv_prev)
    kernel = functools.partial(_prompt_in_kernel, d_model=d, n_heads=n_heads, tk=tk, n_alias=n_alias)
    return pl.pallas_call(
        kernel,
        grid=grid,
        input_output_aliases={N_PROMPT_IN_INPUTS + a: 3 + a for a in range(n_alias)},
        in_specs=[
            row_spec(d),
            pl.BlockSpec((None, None, 1, 3 * d), lambda bi, i: (l, bi, 0, 0)),
            par((1, d)),
            pl.BlockSpec((None, d, 8 * dh), lambda bi, i: (l, 0, 0),
                         pipeline_mode=pl.Buffered(1)),
            par((d, LANES)),
            par((1, LANES)),
            par((1, dh)),
            par((1, dh)),
            par((CONV_W, dh)),
            par((1, dh)),
            pl.BlockSpec(bd.shape, lambda bi, i: (0, 0)),
        ] + [pl.BlockSpec(memory_space=pl.ANY)] * n_alias,
        out_specs=[
            row_spec(dh),
            row_spec(dh),
            pl.BlockSpec((None, tm // tk, dh, tk), lambda bi, i: (bi, i, 0, 0)),
            stacked_spec,
            stacked_spec,
            col_spec(n_heads),
            row_spec(dh),
            row_spec(dh),
            pl.BlockSpec((None, CONV_W - 1, dh), lambda bi, i: (bi, 0, 0)),
        ],
        out_shape=[
            jax.ShapeDtypeStruct((b, s, dh), jnp.bfloat16),
            jax.ShapeDtypeStruct((b, s, dh), jnp.bfloat16),
            jax.ShapeDtypeStruct((b, nk, dh, tk), jnp.bfloat16),
            jax.ShapeDtypeStruct((depth, b, dh, s), jnp.float32),
            jax.ShapeDtypeStruct((depth, b, dh, s), jnp.float32),
            jax.ShapeDtypeStruct((b, n_heads, s), jnp.float32),
            jax.ShapeDtypeStruct((b, s, dh), jnp.float32),
            jax.ShapeDtypeStruct((b, s, dh), jnp.bfloat16),
            jax.ShapeDtypeStruct((b, CONV_W - 1, dh), jnp.float32),
        ],
        scratch_shapes=[pltpu.VMEM((SUBLANES, dh), jnp.float32)],
        compiler_params=pltpu.CompilerParams(
            dimension_semantics=("arbitrary", "arbitrary"),
            vmem_limit_bytes=56 * 1024 * 1024),
        name="prompt_in",
    )(x, mods, norm_g, w_main, w_f, b_f, qg, kg, conv_w, conv_b, bd, *kv_prev)


def _cumsum_kernel(lf_ref, hi_ref, mid_ref, lo_ref):
    x = -_prefix_sum_lanes(lf_ref[...] * LOG2E)
    hi = _bf(x).astype(jnp.float32)
    r = x - hi
    mid = _bf(r).astype(jnp.float32)
    hi_ref[...] = hi
    mid_ref[...] = mid
    lo_ref[...] = r - mid


def _cumsum(lf_t):
    b, h, s = lf_t.shape
    spec = pl.BlockSpec((None, h, s), lambda bi: (bi, 0, 0))
    shape = jax.ShapeDtypeStruct((b, h, s), jnp.float32)
    return pl.pallas_call(
        _cumsum_kernel,
        grid=(b,),
        in_specs=[spec],
        out_specs=[spec, spec, spec],
        out_shape=[shape, shape, shape],
        name="forget_cumsum",
    )(lf_t)


def _attn_kernel(q_ref, k_ref, ka_ref, vt_ref, sza_ref, o_ref,
                 s_scr, tmax_scr, m_scr, acc_scr, *, tq, tk):
    nq = q_ref.shape[0] // tq
    ones_rows = jnp.ones((BF16_ROWS, tk), jnp.bfloat16)
    lane = lax.broadcasted_iota(jnp.int32, (tq, LANES), 1)
    bias_ones = [_bf(jnp.where((lane >= 3 * hh) & (lane < 3 * hh + 3), 1.0, 0.0)) for hh in range(2)]

    def q_rows(qb):
        return pl.ds(pl.multiple_of(qb * tq, tq), tq)

    def q_operands(qb):
        q = q_ref[q_rows(qb), :]
        zero = jnp.zeros_like(q)
        q_split = (jnp.where(lane < HEAD_DIM, q, zero), jnp.where(lane >= HEAD_DIM, q, zero))
        return [jnp.concatenate([q_split[hh], bias_ones[hh]], axis=1) for hh in range(2)]

    def qk(q_heads, j):
        rows = pl.ds(pl.multiple_of(j * tk, tk), tk)
        k_t = jnp.concatenate([k_ref[rows, :], ka_ref[rows, :]], axis=1)
        return [_dot_nt(k_t, q_heads[hh]) for hh in range(2)]

    def scores_to_scratch(s_pair, masked, slot):
        for hh in range(2):
            s_t = s_pair[hh]
            if masked:
                kpos = lax.broadcasted_iota(jnp.int32, s_t.shape, 0)
                qpos = lax.broadcasted_iota(jnp.int32, s_t.shape, 1)
                s_t = jnp.where(kpos <= qpos, s_t, NEG_INF)
            s_scr[slot, hh] = s_t
            col_max = jnp.max(s_t, axis=0, keepdims=True)
            tmax_scr[slot, hh] = jnp.broadcast_to(col_max, (SUBLANES, tq))

    def rep(stat, rows):
        return jnp.broadcast_to(stat[None], (rows // SUBLANES, SUBLANES, tq)).reshape(rows, tq)

    def exps(slot):
        out = []
        for hh in range(2):
            m_old = m_scr[hh]
            m_new = jnp.maximum(m_old, tmax_scr[slot, hh])
            alpha = jnp.exp2(m_old - m_new)
            m_scr[hh] = m_new
            m_rep = rep(m_new, EXP_ROWS)
            p = jnp.concatenate(
                [_bf(jnp.exp2(s_scr[slot, hh, c * EXP_ROWS:(c + 1) * EXP_ROWS, :] - m_rep))
                 for c in range(tk // EXP_ROWS)], axis=0)
            out.append((alpha, p))
        return out

    def pv(j, ap):
        out = []
        for hh in range(2):
            alpha, p = ap[hh]
            v_t = jnp.concatenate(
                [vt_ref[j, hh * HEAD_DIM:(hh + 1) * HEAD_DIM, :], ones_rows], axis=0)
            out.append((alpha, _dot(v_t, p)))
        return out

    def accumulate(av):
        for hh in range(2):
            alpha, pv_t = av[hh]
            acc_scr[hh] = rep(alpha, pv_t.shape[0]) * acc_scr[hh] + pv_t

    def reset_stats():
        m_scr[...] = jnp.full_like(m_scr, NEG_INF)
        acc_scr[...] = jnp.zeros_like(acc_scr)

    scores_to_scratch(qk(q_operands(0), 0), True, 0)
    reset_stats()

    def q_block(qi, carry):
        q_heads = q_operands(qi)

        def step(j_next, slot_next):
            ap = exps(1 - slot_next)
            s_next = qk(q_heads, j_next)
            av = pv(j_next - 1, ap)
            scores_to_scratch(s_next, False, slot_next)
            accumulate(av)

        n_plain = jnp.maximum(qi - 1, 0)

        def body(jj, c):
            step(2 * jj + 1, 1)
            step(2 * jj + 2, 0)
            return c

        lax.fori_loop(0, n_plain // 2, body, 0)

        @pl.when(lax.rem(n_plain, 2) == 1)
        def _():
            step(qi - 1, 1)

        def close(slot, with_diagonal):
            if with_diagonal:
                ap = exps(1 - slot)
                s_diag = qk(q_heads, qi)
            s_first = qk(q_operands(jnp.minimum(qi + 1, nq - 1)), 0)
            if with_diagonal:
                av = pv(qi - 1, ap)
                scores_to_scratch(s_diag, True, slot)
                accumulate(av)
            accumulate(pv(qi, exps(slot)))
            o_t = jnp.concatenate(
                [acc_scr[hh, :HEAD_DIM, :] / acc_scr[hh, HEAD_DIM:HEAD_DIM + 1, :]
                 for hh in range(2)], axis=0)
            o_ref[q_rows(qi), :] = _bf(o_t.T * sza_ref[q_rows(qi), :])
            scores_to_scratch(s_first, False, 0)
            reset_stats()

        @pl.when(qi == 0)
        def _():
            close(0, False)

        for parity in range(2):
            @pl.when((qi > 0) & (lax.rem(qi, 2) == parity))
            def _():
                close(parity, True)

        return carry

    lax.fori_loop(0, nq, q_block, 0)


def _attn(qb, kb, ka, vtb, sza, tq, tk):
    b, s, dh = qb.shape
    npair = dh // LANES
    nk = s // tk
    kernel = functools.partial(_attn_kernel, tq=tq, tk=tk)
    seq_spec = pl.BlockSpec((None, s, LANES), lambda bi, hp: (bi, 0, hp))
    return pl.pallas_call(
        kernel,
        grid=(b, npair),
        in_specs=[
            seq_spec,
            seq_spec,
            pl.BlockSpec((None, None, s, LANES), lambda bi, hp: (bi, hp, 0, 0)),
            pl.BlockSpec((None, nk, LANES, tk), lambda bi, hp: (bi, 0, hp, 0)),
            seq_spec,
        ],
        out_specs=seq_spec,
        out_shape=jax.ShapeDtypeStruct((b, s, dh), jnp.bfloat16),
        scratch_shapes=[
            pltpu.VMEM((2, 2, tk, tq), jnp.float32),
            pltpu.VMEM((2, 2, SUBLANES, tq), jnp.float32),
            pltpu.VMEM((2, SUBLANES, tq), jnp.float32),
            pltpu.VMEM((2, HEAD_DIM + BF16_ROWS, tq), jnp.float32),
        ],
        compiler_params=pltpu.CompilerParams(
            dimension_semantics=("arbitrary", "arbitrary"),
            vmem_limit_bytes=48 * 1024 * 1024),
        name="prompt_attn",
    )(qb, kb, ka, vtb, sza)


def _out_kernel(x_ref, oa_ref, oc_ref, w_ref, gate_ref, y_ref):
    dh = oa_ref.shape[-1]
    y = _dot(oa_ref[...], w_ref[:dh, :]) + _dot(oc_ref[...], w_ref[dh:, :])
    y_ref[...] = x_ref[...] + gate_ref[...] * y


def _prompt_out(l, x, oa, oc, w_out, mods, tm):
    b, s, d = x.shape
    dh = oa.shape[-1]
    row_spec = lambda width: pl.BlockSpec((None, tm, width), lambda bi, i: (bi, i, 0))
    return pl.pallas_call(
        _out_kernel,
        grid=(b, s // tm),
        in_specs=[
            row_spec(d), row_spec(dh), row_spec(dh),
            pl.BlockSpec((None, 2 * dh, d), lambda bi, i: (l, 0, 0)),
            pl.BlockSpec((None, None, 1, d), lambda bi, i: (l, bi, 0, 2)),
        ],
        out_specs=row_spec(d),
        out_shape=jax.ShapeDtypeStruct((b, s, d), jnp.float32),
        compiler_params=pltpu.CompilerParams(
            dimension_semantics=("arbitrary", "arbitrary"),
            vmem_limit_bytes=48 * 1024 * 1024),
        name="prompt_out",
    )(x, oa, oc, w_out, mods)


def _sample_in_kernel(x_ref, mod_ref, st_ref, g_ref, w_ref, wf_ref, bf_ref, qg_ref, kg_ref,
                      cw_ref, cb_ref, bd_ref,
                      q_ref, k_ref, v_ref, lf_ref, sza_ref, oc_ref, hist_ref, *, d_model):
    mod = mod_ref[...]
    shift = mod[:, :d_model]
    scale = mod[:, d_model:2 * d_model]
    q, k, v, sza, logf, bc, cu, zc = _inproj_math(
        x_ref[...], shift, scale, g_ref[...], w_ref, wf_ref, bf_ref[...],
        qg_ref[...], kg_ref[...], bd_ref[...])
    dh = q.shape[-1]
    q_ref[...] = q * (LOG2E * HEAD_DIM ** -0.5)
    k_ref[...] = k
    v_ref[...] = v
    lf_ref[...] = logf
    sza_ref[...] = sza
    st = st_ref[...]
    h0 = st[:, :dh]
    h1 = st[:, dh:]
    cw = cw_ref[...]
    conv = cb_ref[...] + h0 * cw[0:1, :] + h1 * cw[1:2, :] + cu * cw[2:3, :]
    oc_ref[...] = _bf(bc * conv * _silu(zc))
    hist_ref[...] = jnp.concatenate([h1, cu], axis=1)


def _sample_in(l, x, mod_s, st, norm_g, w_main, w_f, b_f, qg, kg, conv_w, conv_b, bd):
    r, d = x.shape
    dh = qg.shape[-1]
    full = lambda shape: pl.BlockSpec(shape, lambda i: (0,) * len(shape))
    par = lambda shape: pl.BlockSpec((None,) + shape, lambda i: (l,) + (0,) * len(shape))
    kernel = functools.partial(_sample_in_kernel, d_model=d)
    f32 = jnp.float32
    return pl.pallas_call(
        kernel,
        grid=(1,),
        in_specs=[
            full((r, d)), par((r, 3 * d)), par((r, 2 * dh)), par((1, d)),
            par((d, 8 * dh)), par((d, LANES)), par((1, LANES)), par((1, dh)), par((1, dh)),
            par((CONV_W, dh)), par((1, dh)), full(bd.shape),
        ],
        out_specs=[
            full((r, dh)), full((r, dh)), full((r, dh)), full((r, LANES)), full((r, dh)),
            full((r, dh)), full((r, 2 * dh)),
        ],
        out_shape=[
            jax.ShapeDtypeStruct((r, dh), f32),
            jax.ShapeDtypeStruct((r, dh), f32),
            jax.ShapeDtypeStruct((r, dh), f32),
            jax.ShapeDtypeStruct((r, LANES), f32),
            jax.ShapeDtypeStruct((r, dh), f32),
            jax.ShapeDtypeStruct((r, dh), jnp.bfloat16),
            jax.ShapeDtypeStruct((r, 2 * dh), f32),
        ],
        compiler_params=pltpu.CompilerParams(vmem_limit_bytes=56 * 1024 * 1024),
        name="sample_in",
    )(x, mod_s, st, norm_g, w_main, w_f, b_f, qg, kg, conv_w, conv_b, bd)


def _sample_attn_kernel(pt_ref, q_ref, kn_ref, vn_ref, lfn_ref, k_hbm, v_hbm, lf_hbm,
                        o_ref, kbuf, vbuf, lfbuf, sems, acc_ref, m_ref, l_ref, c_ref,
                        *, layer_off, n_pages, cpp):
    n_heads, hd, n_tok = kbuf.shape[1], kbuf.shape[2], kbuf.shape[3]
    dh = n_heads * hd
    page = n_tok // cpp
    nch = n_pages // cpp
    g = pl.program_id(0)
    nsteps = pl.num_programs(0)
    slot = lax.rem(g, 2)
    seq = g // nch
    ch = lax.rem(g, nch)

    def copies(step, sl):
        sq = step // nch
        c0 = lax.rem(step, nch) * cpp
        out = []
        for p in range(cpp):
            pid = pt_ref[sq * n_pages + c0 + p] + layer_off
            lanes = pl.ds(p * page, page)
            out.append(pltpu.make_async_copy(k_hbm.at[pid], kbuf.at[sl, :, :, lanes], sems.at[sl, 0]))
            out.append(pltpu.make_async_copy(v_hbm.at[pid], vbuf.at[sl, :, :, lanes], sems.at[sl, 1]))
            out.append(pltpu.make_async_copy(lf_hbm.at[pid], lfbuf.at[sl, :, lanes], sems.at[sl, 2]))
        return out

    @pl.when(g == 0)
    def _():
        o_ref[...] = jnp.zeros_like(o_ref)
        for cp in copies(0, 0):
            cp.start()

    @pl.when(g + 1 < nsteps)
    def _():
        for cp in copies(g + 1, 1 - slot):
            cp.start()

    for cp in copies(g, slot):
        cp.wait()

    @pl.when(ch == 0)
    def _():
        acc_ref[...] = jnp.zeros_like(acc_ref)
        m_ref[...] = jnp.full_like(m_ref, NEG_INF)
        l_ref[...] = jnp.zeros_like(l_ref)
        c_ref[...] = jnp.zeros_like(c_ref)

    qrow = q_ref[pl.ds(seq, 1), :]
    sub = lax.broadcasted_iota(jnp.int32, (BF16_ROWS, dh), 0)
    lane_head = lax.broadcasted_iota(jnp.int32, (BF16_ROWS, dh), 1) // hd
    diag = sub == lane_head
    qbd = jnp.where(diag, qrow, 0.0)

    kc = _bf(kbuf[slot].reshape(dh, n_tok))
    s = _dot(_bf(qbd), kc)[:n_heads]
    cum = _prefix_sum_lanes(lfbuf[slot] * LOG2E) + c_ref[...]
    c_ref[...] = cum[:, n_tok - 1:n_tok]
    logit = s - cum
    m_old = m_ref[...]
    m_new = jnp.maximum(m_old, jnp.max(logit, axis=1, keepdims=True))
    alpha = jnp.exp2(m_old - m_new)
    p_t = jnp.exp2(logit - m_new)
    l_ref[...] = alpha * l_ref[...] + jnp.sum(p_t, axis=1, keepdims=True)
    m_ref[...] = m_new
    p16 = _bf(jnp.concatenate([p_t, jnp.zeros_like(p_t)], axis=0))
    vc = _bf(vbuf[slot].reshape(dh, n_tok))
    acc_ref[...] = alpha * acc_ref[...] + _dot_nt(p16, vc)[:n_heads]

    @pl.when(ch == nch - 1)
    def _():
        krow = kn_ref[pl.ds(seq, 1), :]
        vrow = vn_ref[pl.ds(seq, 1), :]
        lfrow = lfn_ref[pl.ds(seq, 1), :]
        sub8 = lax.broadcasted_iota(jnp.int32, (n_heads, LANES), 0)
        lane8 = lax.broadcasted_iota(jnp.int32, (n_heads, LANES), 1)
        lfcol = jnp.sum(jnp.where(sub8 == lane8, lfrow, 0.0), axis=1, keepdims=True) * LOG2E
        s_self = jnp.sum(qbd[:n_heads] * krow, axis=1, keepdims=True) - (c_ref[...] + lfcol)
        m_prev = m_ref[...]
        m_fin = jnp.maximum(m_prev, s_self)
        a_fin = jnp.exp2(m_prev - m_fin)
        p_self = jnp.exp2(s_self - m_fin)
        l_fin = a_fin * l_ref[...] + p_self
        o = (a_fin * acc_ref[...] + p_self * vrow) / l_fin
        o_ref[pl.ds(seq, 1), :] = jnp.sum(jnp.where(diag[:n_heads], o, 0.0), axis=0, keepdims=True)


def _sample_attn(layer_off, page_table, q, kn, vn, lfn, ck, cv, clf, n_seq):
    r, dh = q.shape
    n_heads = dh // HEAD_DIM
    n_pages = page_table.shape[1]
    cpp = min(PAGES_PER_CHUNK, n_pages)
    nch = n_pages // cpp
    page = ck.shape[-1]
    kernel = functools.partial(
        _sample_attn_kernel, layer_off=layer_off, n_pages=n_pages, cpp=cpp)
    full = lambda shape: pl.BlockSpec(shape, lambda g, pt: (0,) * len(shape))
    anyspec = pl.BlockSpec(memory_space=pl.ANY)
    grid_spec = pltpu.PrefetchScalarGridSpec(
        num_scalar_prefetch=1,
        grid=(n_seq * nch,),
        in_specs=[full((r, dh)), full((r, dh)), full((r, dh)), full((r, LANES)),
                  anyspec, anyspec, anyspec],
        out_specs=full((r, dh)),
        scratch_shapes=[
            pltpu.VMEM((2, n_heads, HEAD_DIM, cpp * page), jnp.float32),
            pltpu.VMEM((2, n_heads, HEAD_DIM, cpp * page), jnp.float32),
            pltpu.VMEM((2, n_heads, cpp * page), jnp.float32),
            pltpu.SemaphoreType.DMA((2, 3)),
            pltpu.VMEM((n_heads, dh), jnp.float32),
            pltpu.VMEM((n_heads, 1), jnp.float32),
            pltpu.VMEM((n_heads, 1), jnp.float32),
            pltpu.VMEM((n_heads, 1), jnp.float32),
        ],
    )
    return pl.pallas_call(
        kernel,
        grid_spec=grid_spec,
        out_shape=jax.ShapeDtypeStruct((r, dh), jnp.float32),
        compiler_params=pltpu.CompilerParams(
            dimension_semantics=("arbitrary",),
            vmem_limit_bytes=48 * 1024 * 1024),
        name="sample_attn",
    )(page_table.reshape(-1), q, kn, vn, lfn, ck, cv, clf)


def _sample_out_kernel(x_ref, o_ref, sza_ref, oc_ref, w_ref, mod_ref, y_ref, *, d_model):
    dh = oc_ref.shape[-1]
    oa = _bf(o_ref[...] * sza_ref[...])
    y = _dot(oa, w_ref[:dh, :]) + _dot(oc_ref[...], w_ref[dh:, :])
    gate = mod_ref[:, 2 * d_model:]
    y_ref[...] = x_ref[...] + gate * y


def _sample_out(l, x, o, sza, oc, w_out, mod_s):
    r, d = x.shape
    dh = oc.shape[-1]
    full = lambda shape: pl.BlockSpec(shape, lambda i: (0,) * len(shape))
    par = lambda shape: pl.BlockSpec((None,) + shape, lambda i: (l,) + (0,) * len(shape))
    return pl.pallas_call(
        functools.partial(_sample_out_kernel, d_model=d),
        grid=(1,),
        in_specs=[full((r, d)), full((r, dh)), full((r, dh)), full((r, dh)),
                  par((2 * dh, d)), par((r, 3 * d))],
        out_specs=full((r, d)),
        out_shape=jax.ShapeDtypeStruct((r, d), jnp.float32),
        name="sample_out",
    )(x, o, sza, oc, w_out, mod_s)


def kernel(x_prompt, x_sample, cache_k, cache_v, cache_logf, state_conv, page_table, c_prompt, c_sample,
           w_in, b_forget, w_out, norm_g, q_gain, k_gain, conv_w, conv_b, w_ada, b_ada):
    depth, d, _ = w_in.shape
    b, s, _ = x_prompt.shape
    n_seq = x_sample.shape[0]
    n_heads = b_forget.shape[-1]
    dh = n_heads * HEAD_DIM
    n_phys, page = cache_k.shape[1], cache_k.shape[2]
    tm = min(512, s)
    tq = tk = min(512, s)
    r_pad = -(-n_seq // SUBLANES) * SUBLANES

    f0 = 4 * dh
    w_t = jnp.swapaxes(w_in, 1, 2)
    w_main = _repack_w_in(w_t, dh, n_heads)
    w_f = _bf(jnp.pad(w_in[:, :, f0:f0 + n_heads], ((0, 0), (0, 0), (0, LANES - n_heads))))
    b_f = jnp.pad(b_forget, ((0, 0), (0, LANES - n_heads))).reshape(depth, 1, LANES)
    w_out_b = _bf(w_out)
    qg = jnp.tile(q_gain, (1, n_heads)).reshape(depth, 1, dh)
    kg = jnp.tile(k_gain, (1, n_heads)).reshape(depth, 1, dh)
    g3 = norm_g.reshape(depth, 1, d)
    cb3 = conv_b.reshape(depth, 1, dh)
    half = dh // 2
    hid = jnp.arange(half) // HEAD_DIM
    bd = _bf(jnp.where(hid[:, None] == hid[None, :], 1.0 / HEAD_DIM, 0.0))

    n_prompt = c_prompt.shape[0]
    c_all = jnp.concatenate([c_prompt, c_sample], axis=0)
    rows = c_all.shape[0]
    rows_pad = -(-rows // SUBLANES) * SUBLANES
    c_all = jnp.pad(c_all, ((0, rows_pad - rows), (0, 0)))
    mods = _ada(c_all, w_ada, b_ada)
    mods4 = mods.reshape(depth, rows_pad, 1, 3 * d)
    mod_s = jnp.pad(mods[:, n_prompt:n_prompt + n_seq, :], ((0, 0), (0, r_pad - n_seq), (0, 0)))

    ck = jnp.transpose(cache_k, (0, 1, 3, 4, 2)).reshape(depth * n_phys, n_heads, HEAD_DIM, page)
    cv = jnp.transpose(cache_v, (0, 1, 3, 4, 2)).reshape(depth * n_phys, n_heads, HEAD_DIM, page)
    clf = jnp.transpose(cache_logf, (0, 1, 3, 2)).reshape(depth * n_phys, n_heads, page)
    st_all = jnp.pad(state_conv.reshape(depth, n_seq, (CONV_W - 1) * dh),
                     ((0, 0), (0, r_pad - n_seq), (0, 0)))

    yp = x_prompt
    lp, cp = [], []
    kv_all = ()
    for l in range(depth):
        qb, kb, vtb, kt_all, vt_all, lft, sza, oc, hist = _prompt_in(
            l, yp, mods4, g3, w_main, w_f, b_f, qg, kg, conv_w, cb3, bd, tm, tk, kv_all)
        kv_all = (kt_all, vt_all)
        parts = jnp.stack(_cumsum(lft), axis=-1)
        parts = parts.reshape(b, n_heads // 2, 2, s, 3)
        ka = jnp.transpose(parts, (0, 1, 3, 2, 4)).reshape(b, n_heads // 2, s, 6)
        ka = _bf(jnp.pad(ka, ((0, 0), (0, 0), (0, 0), (0, LANES - 6))))
        oa = _attn(qb, kb, ka, vtb, sza, tq, tk)
        yp = _prompt_out(l, yp, oa, oc, w_out_b, mods4, min(1024, s))
        lp.append(jnp.transpose(lft, (0, 2, 1)))
        cp.append(hist)
    k_prompt, v_prompt = (
        jnp.transpose(a.reshape(depth, b, n_heads, HEAD_DIM, s), (0, 1, 4, 2, 3)) for a in kv_all)

    ys = jnp.pad(x_sample.reshape(n_seq, d), ((0, r_pad - n_seq), (0, 0)))
    ksm, vsm, lsm, csm = [], [], [], []
    for l in range(depth):
        q, k, v, lf, sza, oc, hist = _sample_in(
            l, ys, mod_s, st_all, g3, w_main, w_f, b_f, qg, kg, conv_w, cb3, bd)
        o = _sample_attn(l * n_phys, page_table, q, k, v, lf, ck, cv, clf, n_seq)
        ys = _sample_out(l, ys, o, sza, oc, w_out_b, mod_s)
        ksm.append(k[:n_seq].reshape(n_seq, 1, n_heads, HEAD_DIM))
        vsm.append(v[:n_seq].reshape(n_seq, 1, n_heads, HEAD_DIM))
        lsm.append(lf[:n_seq, :n_heads].reshape(n_seq, 1, n_heads))
        csm.append(hist[:n_seq].reshape(n_seq, CONV_W - 1, dh))

    return (yp, ys[:n_seq].reshape(n_seq, 1, d), k_prompt, v_prompt, jnp.stack(lp),
            jnp.stack(cp), jnp.stack(ksm), jnp.stack(vsm), jnp.stack(lsm), jnp.stack(csm))
```

```python
import functools

import jax
import jax.numpy as jnp
from jax import lax
from jax.experimental import pallas as pl
from jax.experimental.pallas import tpu as pltpu

HEAD_DIM = 64
CONV_W = 3
RMS_EPS = 1e-6
LANES = 128
SUBLANES = 8
BF16_ROWS = 16
LOG2E = 1.4426950408889634
NEG_INF = float("-inf")
PAGES_PER_CHUNK = 16
EXP_ROWS = 128
N_PROMPT_IN_INPUTS = 11

_NT = (((1,), (1,)), ((), ()))


def _bf(x):
    return x.astype(jnp.bfloat16)


def _dot(a, b):
    return jnp.dot(a, b, preferred_element_type=jnp.float32)


def _dot_nt(a, b):
    return lax.dot_general(a, b, _NT, preferred_element_type=jnp.float32)


def _silu(x):
    return x * (1.0 / (1.0 + jnp.exp(-x)))


def _log_sigmoid(z):
    return jnp.minimum(z, 0.0) - jnp.log(1.0 + jnp.exp(-jnp.abs(z)))


def _prefix_sum_lanes(x):
    n = x.shape[-1]
    lane = lax.broadcasted_iota(jnp.int32, x.shape, x.ndim - 1)
    sh = 1
    while sh < n:
        x = x + jnp.where(lane >= sh, pltpu.roll(x, sh, x.ndim - 1), 0.0)
        sh *= 2
    return x


def _ada_kernel(c_ref, w_ref, b_ref, o_ref):
    c = c_ref[...]
    o_ref[...] = _dot(_bf(_silu(c)), _bf(w_ref[...])) + b_ref[...]


def _ada(c_all, w_ada, b_ada):
    depth, d, n3 = w_ada.shape
    rows = c_all.shape[0]
    bn = 768
    return pl.pallas_call(
        _ada_kernel,
        grid=(depth, n3 // bn),
        in_specs=[
            pl.BlockSpec((rows, d), lambda l, j: (0, 0)),
            pl.BlockSpec((None, d, bn), lambda l, j: (l, 0, j)),
            pl.BlockSpec((None, 1, bn), lambda l, j: (l, 0, j)),
        ],
        out_specs=pl.BlockSpec((None, rows, bn), lambda l, j: (l, 0, j)),
        out_shape=jax.ShapeDtypeStruct((depth, rows, n3), jnp.float32),
        name="ada_mod",
    )(c_all, w_ada, b_ada.reshape(depth, 1, n3))


def _repack_kernel(wt_ref, o_ref):
    o_ref[...] = _bf(wt_ref[0].T)


def _repack_w_in(w_t, dh, n_skip):
    depth, _, d = w_t.shape
    n_groups = 8

    def src_rows(l, c):
        row = c * dh + jnp.where(c >= n_groups // 2, n_skip, 0)
        return (l, pl.multiple_of(row, SUBLANES), 0)

    return pl.pallas_call(
        _repack_kernel,
        grid=(depth, n_groups),
        in_specs=[pl.BlockSpec((pl.Element(1), pl.Element(dh), pl.Element(d)), src_rows)],
        out_specs=pl.BlockSpec((None, d, dh), lambda l, c: (l, 0, c)),
        out_shape=jax.ShapeDtypeStruct((depth, d, n_groups * dh), jnp.bfloat16),
        name="repack_w_in",
    )(w_t)


def _inproj_math(x, shift, scale, g, w_ref, wf_ref, bf_row, qg, kg, bd):
    dh = qg.shape[-1]
    ms = jnp.mean(x * x, axis=-1, keepdims=True)
    xn = x * lax.rsqrt(ms + RMS_EPS)
    h = _bf(xn * g * (1.0 + scale) + shift)

    def col(c):
        return _dot(h, w_ref[:, c * dh:(c + 1) * dh])

    def head_norm(t, gain):
        t2 = _bf(t * t)
        half = dh // 2
        msq = jnp.concatenate(
            [_dot(t2[:, :half], bd), _dot(t2[:, half:], bd)], axis=1)
        return t * lax.rsqrt(msq + RMS_EPS) * gain

    q = head_norm(col(0), qg)
    k = head_norm(col(1), kg)
    v = col(2)
    sza = _silu(col(3))
    fl = _dot(h, wf_ref[...]) + bf_row
    logf = _log_sigmoid(fl)
    bc = col(4)
    cu = col(5) * col(6)
    zc = col(7)
    return q, k, v, sza, logf, bc, cu, zc


def _prompt_in_kernel(*refs, d_model, n_heads, tk, n_alias):
    (x_ref, mod_ref, g_ref, w_ref, wf_ref, bf_ref, qg_ref, kg_ref,
     cw_ref, cb_ref, bd_ref) = refs[:N_PROMPT_IN_INPUTS]
    (qb_ref, kb_ref, vtb_ref, kt_ref, vt_ref, lft_ref, sza_ref, oc_ref,
     hist_ref, carry_ref) = refs[N_PROMPT_IN_INPUTS + n_alias:]
    i = pl.program_id(1)
    tm = x_ref.shape[0]
    mod = mod_ref[...]
    shift = mod[:, :d_model]
    scale = mod[:, d_model:2 * d_model]
    q, k, v, sza, logf, bc, cu, zc = _inproj_math(
        x_ref[...], shift, scale, g_ref[...], w_ref, wf_ref, bf_ref[...],
        qg_ref[...], kg_ref[...], bd_ref[...])

    qb_ref[...] = _bf(q * (LOG2E * HEAD_DIM ** -0.5))
    kb_ref[...] = _bf(k)
    k_t = k.T
    v_t = v.T
    if n_alias:
        kt_ref[...] = k_t
        vt_ref[...] = v_t
    else:
        for slot in range(kt_ref.shape[0]):
            kt_ref[slot] = k_t
            vt_ref[slot] = v_t
    for c in range(tm // tk):
        vtb_ref[c] = _bf(v_t[:, c * tk:(c + 1) * tk])
    sza_ref[...] = sza
    lft_ref[...] = logf.T[:n_heads, :]

    @pl.when(i == 0)
    def _():
        carry_ref[...] = jnp.zeros_like(carry_ref)

    prev = carry_ref[...]
    row = lax.broadcasted_iota(jnp.int32, cu.shape, 0)
    s1 = pltpu.roll(cu, 1, 0)
    s1 = jnp.where(row == 0, prev[7:8, :], s1)
    s2 = pltpu.roll(cu, 2, 0)
    s2 = jnp.where(row == 0, prev[6:7, :], jnp.where(row == 1, prev[7:8, :], s2))
    cw = cw_ref[...]
    conv = cb_ref[...] + s2 * cw[0:1, :] + s1 * cw[1:2, :] + cu * cw[2:3, :]
    oc_ref[...] = _bf(bc * conv * _silu(zc))
    carry_ref[...] = cu[tm - SUBLANES:, :]

    @pl.when(i == pl.num_programs(1) - 1)
    def _():
        hist_ref[...] = cu[tm - (CONV_W - 1):, :]


def _prompt_in(l, x, mods, norm_g, w_main, w_f, b_f, qg, kg, conv_w, conv_b, bd, tm, tk, kv_prev):
    depth = w_main.shape[0]
    b, s, d = x.shape
    dh = qg.shape[-1]
    n_heads = dh // HEAD_DIM
    nk = s // tk
    grid = (b, s // tm)
    row_spec = lambda width: pl.BlockSpec((None, tm, width), lambda bi, i: (bi, i, 0))
    col_spec = lambda height: pl.BlockSpec((None, height, tm), lambda bi, i: (bi, 0, i))
    if kv_prev:
        stacked_spec = pl.BlockSpec((None, None, dh, tm), lambda bi, i: (l, bi, 0, i))
    else:
        stacked_spec = pl.BlockSpec((depth, None, dh, tm), lambda bi, i: (0, bi, 0, i))
    par = lambda shape: pl.BlockSpec((None,) + shape, lambda bi, i: (l,) + (0,) * len(shape))
    n_alias = len(kv_prev)
    kernel = functools.partial(_prompt_in_kernel, d_model=d, n_heads=n_heads, tk=tk, n_alias=n_alias)
    return pl.pallas_call(
        kernel,
        grid=grid,
        input_output_aliases={N_PROMPT_IN_INPUTS + a: 3 + a for a in range(n_alias)},
        in_specs=[
            row_spec(d),
            pl.BlockSpec((None, None, 1, 3 * d), lambda bi, i: (l, bi, 0, 0)),
            par((1, d)),
            pl.BlockSpec((None, d, 8 * dh), lambda bi, i: (l, 0, 0),
                         pipeline_mode=pl.Buffered(1)),
            par((d, LANES)),
            par((1, LANES)),
            par((1, dh)),
            par((1, dh)),
            par((CONV_W, dh)),
            par((1, dh)),
            pl.BlockSpec(bd.shape, lambda bi, i: (0, 0)),
        ] + [pl.BlockSpec(memory_space=pl.ANY)] * n_alias,
        out_specs=[
            row_spec(dh),
            row_spec(dh),
            pl.BlockSpec((None, tm // tk, dh, tk), lambda bi, i: (bi, i, 0, 0)),
            stacked_spec,
            stacked_spec,
            col_spec(n_heads),
            row_spec(dh),
            row_spec(dh),
            pl.BlockSpec((None, CONV_W - 1, dh), lambda bi, i: (bi, 0, 0)),
        ],
        out_shape=[
            jax.ShapeDtypeStruct((b, s, dh), jnp.bfloat16),
            jax.ShapeDtypeStruct((b, s, dh), jnp.bfloat16),
            jax.ShapeDtypeStruct((b, nk, dh, tk), jnp.bfloat16),
            jax.ShapeDtypeStruct((depth, b, dh, s), jnp.float32),
            jax.ShapeDtypeStruct((depth, b, dh, s), jnp.float32),
            jax.ShapeDtypeStruct((b, n_heads, s), jnp.float32),
            jax.ShapeDtypeStruct((b, s, dh), jnp.float32),
            jax.ShapeDtypeStruct((b, s, dh), jnp.bfloat16),
            jax.ShapeDtypeStruct((b, CONV_W - 1, dh), jnp.float32),
        ],
        scratch_shapes=[pltpu.VMEM((SUBLANES, dh), jnp.float32)],
        compiler_params=pltpu.CompilerParams(
            dimension_semantics=("arbitrary", "arbitrary"),
            vmem_limit_bytes=56 * 1024 * 1024),
        name="prompt_in",
    )(x, mods, norm_g, w_main, w_f, b_f, qg, kg, conv_w, conv_b, bd, *kv_prev)


def _cumsum_kernel(lf_ref, ka_ref):
    n_heads, s = lf_ref.shape
    x = -_prefix_sum_lanes(lf_ref[...] * LOG2E)
    hi = _bf(x).astype(jnp.float32)
    r = x - hi
    mid = _bf(r).astype(jnp.float32)
    lo = r - mid
    pad = jnp.zeros((LANES - 3 * n_heads, s), jnp.float32)
    ka_ref[...] = _bf(jnp.concatenate([hi, mid, lo, pad], axis=0).T)


def _cumsum(lf_t):
    b, h, s = lf_t.shape
    return pl.pallas_call(
        _cumsum_kernel,
        grid=(b,),
        in_specs=[pl.BlockSpec((None, h, s), lambda bi: (bi, 0, 0))],
        out_specs=pl.BlockSpec((None, s, LANES), lambda bi: (bi, 0, 0)),
        out_shape=jax.ShapeDtypeStruct((b, s, LANES), jnp.bfloat16),
        name="forget_cumsum",
    )(lf_t)


def _attn_kernel(q_ref, k_ref, ka_ref, vt_ref, sza_ref, o_ref,
                 s_scr, tmax_scr, m_scr, acc_scr, *, tq, tk, n_heads):
    nq = q_ref.shape[0] // tq
    ones_rows = jnp.ones((BF16_ROWS, tk), jnp.bfloat16)
    lane = lax.broadcasted_iota(jnp.int32, (tq, LANES), 1)
    bias_ones = []
    for hh in range(2):
        head = 2 * pl.program_id(1) + hh
        mine = (lane < 3 * n_heads) & (lax.rem(lane, n_heads) == head)
        bias_ones.append(_bf(jnp.where(mine, 1.0, 0.0)))

    def q_rows(qb):
        return pl.ds(pl.multiple_of(qb * tq, tq), tq)

    def q_operands(qb):
        q = q_ref[q_rows(qb), :]
        zero = jnp.zeros_like(q)
        q_split = (jnp.where(lane < HEAD_DIM, q, zero), jnp.where(lane >= HEAD_DIM, q, zero))
        return [jnp.concatenate([q_split[hh], bias_ones[hh]], axis=1) for hh in range(2)]

    def qk(q_heads, j):
        rows = pl.ds(pl.multiple_of(j * tk, tk), tk)
        k_t = jnp.concatenate([k_ref[rows, :], ka_ref[rows, :]], axis=1)
        return [_dot_nt(k_t, q_heads[hh]) for hh in range(2)]

    def scores_to_scratch(s_pair, masked, slot):
        for hh in range(2):
            s_t = s_pair[hh]
            if masked:
                kpos = lax.broadcasted_iota(jnp.int32, s_t.shape, 0)
                qpos = lax.broadcasted_iota(jnp.int32, s_t.shape, 1)
                s_t = jnp.where(kpos <= qpos, s_t, NEG_INF)
            s_scr[slot, hh] = s_t
            col_max = jnp.max(s_t, axis=0, keepdims=True)
            tmax_scr[slot, hh] = jnp.broadcast_to(col_max, (SUBLANES, tq))

    def rep(stat, rows):
        return jnp.broadcast_to(stat[None], (rows // SUBLANES, SUBLANES, tq)).reshape(rows, tq)

    def exps(slot):
        out = []
        for hh in range(2):
            m_old = m_scr[hh]
            m_new = jnp.maximum(m_old, tmax_scr[slot, hh])
            alpha = jnp.exp2(m_old - m_new)
            m_scr[hh] = m_new
            m_rep = rep(m_new, EXP_ROWS)
            p = jnp.concatenate(
                [_bf(jnp.exp2(s_scr[slot, hh, c * EXP_ROWS:(c + 1) * EXP_ROWS, :] - m_rep))
                 for c in range(tk // EXP_ROWS)], axis=0)
            out.append((alpha, p))
        return out

    def pv(j, ap):
        out = []
        for hh in range(2):
            alpha, p = ap[hh]
            v_t = jnp.concatenate(
                [vt_ref[j, hh * HEAD_DIM:(hh + 1) * HEAD_DIM, :], ones_rows], axis=0)
            out.append((alpha, _dot(v_t, p)))
        return out

    def accumulate(av):
        for hh in range(2):
            alpha, pv_t = av[hh]
            acc_scr[hh] = rep(alpha, pv_t.shape[0]) * acc_scr[hh] + pv_t

    def reset_stats():
        m_scr[...] = jnp.full_like(m_scr, NEG_INF)
        acc_scr[...] = jnp.zeros_like(acc_scr)

    scores_to_scratch(qk(q_operands(0), 0), True, 0)
    reset_stats()

    def q_block(qi, carry):
        q_heads = q_operands(qi)

        def step(j_next, slot_next):
            ap = exps(1 - slot_next)
            s_next = qk(q_heads, j_next)
            av = pv(j_next - 1, ap)
            scores_to_scratch(s_next, False, slot_next)
            accumulate(av)

        n_plain = jnp.maximum(qi - 1, 0)

        def body(jj, c):
            step(2 * jj + 1, 1)
            step(2 * jj + 2, 0)
            return c

        lax.fori_loop(0, n_plain // 2, body, 0)

        @pl.when(lax.rem(n_plain, 2) == 1)
        def _():
            step(qi - 1, 1)

        def close(slot, with_diagonal):
            if with_diagonal:
                ap = exps(1 - slot)
                s_diag = qk(q_heads, qi)
            s_first = qk(q_operands(jnp.minimum(qi + 1, nq - 1)), 0)
            if with_diagonal:
                av = pv(qi - 1, ap)
                scores_to_scratch(s_diag, True, slot)
                accumulate(av)
            accumulate(pv(qi, exps(slot)))
            o_t = jnp.concatenate(
                [acc_scr[hh, :HEAD_DIM, :] / acc_scr[hh, HEAD_DIM:HEAD_DIM + 1, :]
                 for hh in range(2)], axis=0)
            o_ref[q_rows(qi), :] = _bf(o_t.T * sza_ref[q_rows(qi), :])
            scores_to_scratch(s_first, False, 0)
            reset_stats()

        @pl.when(qi == 0)
        def _():
            close(0, False)

        for parity in range(2):
            @pl.when((qi > 0) & (lax.rem(qi, 2) == parity))
            def _():
                close(parity, True)

        return carry

    lax.fori_loop(0, nq, q_block, 0)


def _attn(qb, kb, ka, vtb, sza, tq, tk):
    b, s, dh = qb.shape
    npair = dh // LANES
    nk = s // tk
    kernel = functools.partial(_attn_kernel, tq=tq, tk=tk, n_heads=dh // HEAD_DIM)
    seq_spec = pl.BlockSpec((None, s, LANES), lambda bi, hp: (bi, 0, hp))
    return pl.pallas_call(
        kernel,
        grid=(b, npair),
        in_specs=[
            seq_spec,
            seq_spec,
            pl.BlockSpec((None, s, LANES), lambda bi, hp: (bi, 0, 0)),
            pl.BlockSpec((None, nk, LANES, tk), lambda bi, hp: (bi, 0, hp, 0)),
            seq_spec,
        ],
        out_specs=seq_spec,
        out_shape=jax.ShapeDtypeStruct((b, s, dh), jnp.bfloat16),
        scratch_shapes=[
            pltpu.VMEM((2, 2, tk, tq), jnp.float32),
            pltpu.VMEM((2, 2, SUBLANES, tq), jnp.float32),
            pltpu.VMEM((2, SUBLANES, tq), jnp.float32),
            pltpu.VMEM((2, HEAD_DIM + BF16_ROWS, tq), jnp.float32),
        ],
        compiler_params=pltpu.CompilerParams(
            dimension_semantics=("arbitrary", "arbitrary"),
            vmem_limit_bytes=48 * 1024 * 1024),
        name="prompt_attn",
    )(qb, kb, ka, vtb, sza)


def _out_kernel(x_ref, oa_ref, oc_ref, w_ref, gate_ref, y_ref):
    dh = oa_ref.shape[-1]
    y = _dot(oa_ref[...], w_ref[:dh, :]) + _dot(oc_ref[...], w_ref[dh:, :])
    y_ref[...] = x_ref[...] + gate_ref[...] * y


def _prompt_out(l, x, oa, oc, w_out, mods, tm):
    b, s, d = x.shape
    dh = oa.shape[-1]
    row_spec = lambda width: pl.BlockSpec((None, tm, width), lambda bi, i: (bi, i, 0))
    return pl.pallas_call(
        _out_kernel,
        grid=(b, s // tm),
        in_specs=[
            row_spec(d), row_spec(dh), row_spec(dh),
            pl.BlockSpec((None, 2 * dh, d), lambda bi, i: (l, 0, 0)),
            pl.BlockSpec((None, None, 1, d), lambda bi, i: (l, bi, 0, 2)),
        ],
        out_specs=row_spec(d),
        out_shape=jax.ShapeDtypeStruct((b, s, d), jnp.float32),
        compiler_params=pltpu.CompilerParams(
            dimension_semantics=("arbitrary", "arbitrary"),
            vmem_limit_bytes=48 * 1024 * 1024),
        name="prompt_out",
    )(x, oa, oc, w_out, mods)


def _sample_in_kernel(x_ref, mod_ref, st_ref, g_ref, w_ref, wf_ref, bf_ref, qg_ref, kg_ref,
                      cw_ref, cb_ref, bd_ref,
                      q_ref, k_ref, v_ref, lf_ref, sza_ref, oc_ref, hist_ref, *, d_model):
    mod = mod_ref[...]
    shift = mod[:, :d_model]
    scale = mod[:, d_model:2 * d_model]
    q, k, v, sza, logf, bc, cu, zc = _inproj_math(
        x_ref[...], shift, scale, g_ref[...], w_ref, wf_ref, bf_ref[...],
        qg_ref[...], kg_ref[...], bd_ref[...])
    dh = q.shape[-1]
    q_ref[...] = q * (LOG2E * HEAD_DIM ** -0.5)
    k_ref[...] = k
    v_ref[...] = v
    lf_ref[...] = logf
    sza_ref[...] = sza
    st = st_ref[...]
    h0 = st[:, :dh]
    h1 = st[:, dh:]
    cw = cw_ref[...]
    conv = cb_ref[...] + h0 * cw[0:1, :] + h1 * cw[1:2, :] + cu * cw[2:3, :]
    oc_ref[...] = _bf(bc * conv * _silu(zc))
    hist_ref[...] = jnp.concatenate([h1, cu], axis=1)


def _sample_in(l, x, mod_s, st, norm_g, w_main, w_f, b_f, qg, kg, conv_w, conv_b, bd):
    r, d = x.shape
    dh = qg.shape[-1]
    full = lambda shape: pl.BlockSpec(shape, lambda i: (0,) * len(shape))
    par = lambda shape: pl.BlockSpec((None,) + shape, lambda i: (l,) + (0,) * len(shape))
    kernel = functools.partial(_sample_in_kernel, d_model=d)
    f32 = jnp.float32
    return pl.pallas_call(
        kernel,
        grid=(1,),
        in_specs=[
            full((r, d)), par((r, 3 * d)), par((r, 2 * dh)), par((1, d)),
            par((d, 8 * dh)), par((d, LANES)), par((1, LANES)), par((1, dh)), par((1, dh)),
            par((CONV_W, dh)), par((1, dh)), full(bd.shape),
        ],
        out_specs=[
            full((r, dh)), full((r, dh)), full((r, dh)), full((r, LANES)), full((r, dh)),
            full((r, dh)), full((r, 2 * dh)),
        ],
        out_shape=[
            jax.ShapeDtypeStruct((r, dh), f32),
            jax.ShapeDtypeStruct((r, dh), f32),
            jax.ShapeDtypeStruct((r, dh), f32),
            jax.ShapeDtypeStruct((r, LANES), f32),
            jax.ShapeDtypeStruct((r, dh), f32),
            jax.ShapeDtypeStruct((r, dh), jnp.bfloat16),
            jax.ShapeDtypeStruct((r, 2 * dh), f32),
        ],
        compiler_params=pltpu.CompilerParams(vmem_limit_bytes=56 * 1024 * 1024),
        name="sample_in",
    )(x, mod_s, st, norm_g, w_main, w_f, b_f, qg, kg, conv_w, conv_b, bd)


def _sample_attn_kernel(pt_ref, q_ref, kn_ref, vn_ref, lfn_ref, k_hbm, v_hbm, lf_hbm,
                        o_ref, kbuf, vbuf, lfbuf, sems, acc_ref, m_ref, l_ref, c_ref,
                        *, layer_off, n_pages, cpp):
    n_heads, hd, n_tok = kbuf.shape[1], kbuf.shape[2], kbuf.shape[3]
    dh = n_heads * hd
    page = n_tok // cpp
    nch = n_pages // cpp
    g = pl.program_id(0)
    nsteps = pl.num_programs(0)
    slot = lax.rem(g, 2)
    seq = g // nch
    ch = lax.rem(g, nch)

    def copies(step, sl):
        sq = step // nch
        c0 = lax.rem(step, nch) * cpp
        out = []
        for p in range(cpp):
            pid = pt_ref[sq * n_pages + c0 + p] + layer_off
            lanes = pl.ds(p * page, page)
            out.append(pltpu.make_async_copy(k_hbm.at[pid], kbuf.at[sl, :, :, lanes], sems.at[sl, 0]))
            out.append(pltpu.make_async_copy(v_hbm.at[pid], vbuf.at[sl, :, :, lanes], sems.at[sl, 1]))
            out.append(pltpu.make_async_copy(lf_hbm.at[pid], lfbuf.at[sl, :, lanes], sems.at[sl, 2]))
        return out

    @pl.when(g == 0)
    def _():
        o_ref[...] = jnp.zeros_like(o_ref)
        for cp in copies(0, 0):
            cp.start()

    @pl.when(g + 1 < nsteps)
    def _():
        for cp in copies(g + 1, 1 - slot):
            cp.start()

    for cp in copies(g, slot):
        cp.wait()

    @pl.when(ch == 0)
    def _():
        acc_ref[...] = jnp.zeros_like(acc_ref)
        m_ref[...] = jnp.full_like(m_ref, NEG_INF)
        l_ref[...] = jnp.zeros_like(l_ref)
        c_ref[...] = jnp.zeros_like(c_ref)

    qrow = q_ref[pl.ds(seq, 1), :]
    sub = lax.broadcasted_iota(jnp.int32, (BF16_ROWS, dh), 0)
    lane_head = lax.broadcasted_iota(jnp.int32, (BF16_ROWS, dh), 1) // hd
    diag = sub == lane_head
    qbd = jnp.where(diag, qrow, 0.0)

    kc = _bf(kbuf[slot].reshape(dh, n_tok))
    s = _dot(_bf(qbd), kc)[:n_heads]
    cum = _prefix_sum_lanes(lfbuf[slot] * LOG2E) + c_ref[...]
    c_ref[...] = cum[:, n_tok - 1:n_tok]
    logit = s - cum
    m_old = m_ref[...]
    m_new = jnp.maximum(m_old, jnp.max(logit, axis=1, keepdims=True))
    alpha = jnp.exp2(m_old - m_new)
    p_t = jnp.exp2(logit - m_new)
    l_ref[...] = alpha * l_ref[...] + jnp.sum(p_t, axis=1, keepdims=True)
    m_ref[...] = m_new
    p16 = _bf(jnp.concatenate([p_t, jnp.zeros_like(p_t)], axis=0))
    vc = _bf(vbuf[slot].reshape(dh, n_tok))
    acc_ref[...] = alpha * acc_ref[...] + _dot_nt(p16, vc)[:n_heads]

    @pl.when(ch == nch - 1)
    def _():
        krow = kn_ref[pl.ds(seq, 1), :]
        vrow = vn_ref[pl.ds(seq, 1), :]
        lfrow = lfn_ref[pl.ds(seq, 1), :]
        sub8 = lax.broadcasted_iota(jnp.int32, (n_heads, LANES), 0)
        lane8 = lax.broadcasted_iota(jnp.int32, (n_heads, LANES), 1)
        lfcol = jnp.sum(jnp.where(sub8 == lane8, lfrow, 0.0), axis=1, keepdims=True) * LOG2E
        s_self = jnp.sum(qbd[:n_heads] * krow, axis=1, keepdims=True) - (c_ref[...] + lfcol)
        m_prev = m_ref[...]
        m_fin = jnp.maximum(m_prev, s_self)
        a_fin = jnp.exp2(m_prev - m_fin)
        p_self = jnp.exp2(s_self - m_fin)
        l_fin = a_fin * l_ref[...] + p_self
        o = (a_fin * acc_ref[...] + p_self * vrow) / l_fin
        o_ref[pl.ds(seq, 1), :] = jnp.sum(jnp.where(diag[:n_heads], o, 0.0), axis=0, keepdims=True)


def _sample_attn(layer_off, page_table, q, kn, vn, lfn, ck, cv, clf, n_seq):
    r, dh = q.shape
    n_heads = dh // HEAD_DIM
    n_pages = page_table.shape[1]
    cpp = min(PAGES_PER_CHUNK, n_pages)
    nch = n_pages // cpp
    page = ck.shape[-1]
    kernel = functools.partial(
        _sample_attn_kernel, layer_off=layer_off, n_pages=n_pages, cpp=cpp)
    full = lambda shape: pl.BlockSpec(shape, lambda g, pt: (0,) * len(shape))
    anyspec = pl.BlockSpec(memory_space=pl.ANY)
    grid_spec = pltpu.PrefetchScalarGridSpec(
        num_scalar_prefetch=1,
        grid=(n_seq * nch,),
        in_specs=[full((r, dh)), full((r, dh)), full((r, dh)), full((r, LANES)),
                  anyspec, anyspec, anyspec],
        out_specs=full((r, dh)),
        scratch_shapes=[
            pltpu.VMEM((2, n_heads, HEAD_DIM, cpp * page), jnp.float32),
            pltpu.VMEM((2, n_heads, HEAD_DIM, cpp * page), jnp.float32),
            pltpu.VMEM((2, n_heads, cpp * page), jnp.float32),
            pltpu.SemaphoreType.DMA((2, 3)),
            pltpu.VMEM((n_heads, dh), jnp.float32),
            pltpu.VMEM((n_heads, 1), jnp.float32),
            pltpu.VMEM((n_heads, 1), jnp.float32),
            pltpu.VMEM((n_heads, 1), jnp.float32),
        ],
    )
    return pl.pallas_call(
        kernel,
        grid_spec=grid_spec,
        out_shape=jax.ShapeDtypeStruct((r, dh), jnp.float32),
        compiler_params=pltpu.CompilerParams(
            dimension_semantics=("arbitrary",),
            vmem_limit_bytes=48 * 1024 * 1024),
        name="sample_attn",
    )(page_table.reshape(-1), q, kn, vn, lfn, ck, cv, clf)


def _sample_out_kernel(x_ref, o_ref, sza_ref, oc_ref, w_ref, mod_ref, y_ref, *, d_model):
    dh = oc_ref.shape[-1]
    oa = _bf(o_ref[...] * sza_ref[...])
    y = _dot(oa, w_ref[:dh, :]) + _dot(oc_ref[...], w_ref[dh:, :])
    gate = mod_ref[:, 2 * d_model:]
    y_ref[...] = x_ref[...] + gate * y


def _sample_out(l, x, o, sza, oc, w_out, mod_s):
    r, d = x.shape
    dh = oc.shape[-1]
    full = lambda shape: pl.BlockSpec(shape, lambda i: (0,) * len(shape))
    par = lambda shape: pl.BlockSpec((None,) + shape, lambda i: (l,) + (0,) * len(shape))
    return pl.pallas_call(
        functools.partial(_sample_out_kernel, d_model=d),
        grid=(1,),
        in_specs=[full((r, d)), full((r, dh)), full((r, dh)), full((r, dh)),
                  par((2 * dh, d)), par((r, 3 * d))],
        out_specs=full((r, d)),
        out_shape=jax.ShapeDtypeStruct((r, d), jnp.float32),
        name="sample_out",
    )(x, o, sza, oc, w_out, mod_s)


def kernel(x_prompt, x_sample, cache_k, cache_v, cache_logf, state_conv, page_table, c_prompt, c_sample,
           w_in, b_forget, w_out, norm_g, q_gain, k_gain, conv_w, conv_b, w_ada, b_ada):
    depth, d, _ = w_in.shape
    b, s, _ = x_prompt.shape
    n_seq = x_sample.shape[0]
    n_heads = b_forget.shape[-1]
    dh = n_heads * HEAD_DIM
    n_phys, page = cache_k.shape[1], cache_k.shape[2]
    tm = min(512, s)
    tq = tk = min(512, s)
    r_pad = -(-n_seq // SUBLANES) * SUBLANES

    f0 = 4 * dh
    w_t = jnp.swapaxes(w_in, 1, 2)
    w_main = _repack_w_in(w_t, dh, n_heads)
    w_f = _bf(jnp.pad(w_in[:, :, f0:f0 + n_heads], ((0, 0), (0, 0), (0, LANES - n_heads))))
    b_f = jnp.pad(b_forget, ((0, 0), (0, LANES - n_heads))).reshape(depth, 1, LANES)
    w_out_b = _bf(w_out)
    qg = jnp.tile(q_gain, (1, n_heads)).reshape(depth, 1, dh)
    kg = jnp.tile(k_gain, (1, n_heads)).reshape(depth, 1, dh)
    g3 = norm_g.reshape(depth, 1, d)
    cb3 = conv_b.reshape(depth, 1, dh)
    half = dh // 2
    hid = jnp.arange(half) // HEAD_DIM
    bd = _bf(jnp.where(hid[:, None] == hid[None, :], 1.0 / HEAD_DIM, 0.0))

    n_prompt = c_prompt.shape[0]
    c_all = jnp.concatenate([c_prompt, c_sample], axis=0)
    rows = c_all.shape[0]
    rows_pad = -(-rows // SUBLANES) * SUBLANES
    c_all = jnp.pad(c_all, ((0, rows_pad - rows), (0, 0)))
    mods = _ada(c_all, w_ada, b_ada)
    mods4 = mods.reshape(depth, rows_pad, 1, 3 * d)
    mod_s = jnp.pad(mods[:, n_prompt:n_prompt + n_seq, :], ((0, 0), (0, r_pad - n_seq), (0, 0)))

    ck = jnp.transpose(cache_k, (0, 1, 3, 4, 2)).reshape(depth * n_phys, n_heads, HEAD_DIM, page)
    cv = jnp.transpose(cache_v, (0, 1, 3, 4, 2)).reshape(depth * n_phys, n_heads, HEAD_DIM, page)
    clf = jnp.transpose(cache_logf, (0, 1, 3, 2)).reshape(depth * n_phys, n_heads, page)
    st_all = jnp.pad(state_conv.reshape(depth, n_seq, (CONV_W - 1) * dh),
                     ((0, 0), (0, r_pad - n_seq), (0, 0)))

    yp = x_prompt
    lp, cp = [], []
    kv_all = ()
    for l in range(depth):
        qb, kb, vtb, kt_all, vt_all, lft, sza, oc, hist = _prompt_in(
            l, yp, mods4, g3, w_main, w_f, b_f, qg, kg, conv_w, cb3, bd, tm, tk, kv_all)
        kv_all = (kt_all, vt_all)
        ka = _cumsum(lft)
        oa = _attn(qb, kb, ka, vtb, sza, tq, tk)
        yp = _prompt_out(l, yp, oa, oc, w_out_b, mods4, min(1024, s))
        lp.append(jnp.transpose(lft, (0, 2, 1)))
        cp.append(hist)
    k_prompt, v_prompt = (
        jnp.transpose(a.reshape(depth, b, n_heads, HEAD_DIM, s), (0, 1, 4, 2, 3)) for a in kv_all)

    ys = jnp.pad(x_sample.reshape(n_seq, d), ((0, r_pad - n_seq), (0, 0)))
    ksm, vsm, lsm, csm = [], [], [], []
    for l in range(depth):
        q, k, v, lf, sza, oc, hist = _sample_in(
            l, ys, mod_s, st_all, g3, w_main, w_f, b_f, qg, kg, conv_w, cb3, bd)
        o = _sample_attn(l * n_phys, page_table, q, k, v, lf, ck, cv, clf, n_seq)
        ys = _sample_out(l, ys, o, sza, oc, w_out_b, mod_s)
        ksm.append(k[:n_seq].reshape(n_seq, 1, n_heads, HEAD_DIM))
        vsm.append(v[:n_seq].reshape(n_seq, 1, n_heads, HEAD_DIM))
        lsm.append(lf[:n_seq, :n_heads].reshape(n_seq, 1, n_heads))
        csm.append(hist[:n_seq].reshape(n_seq, CONV_W - 1, dh))

    return (yp, ys[:n_seq].reshape(n_seq, 1, d), k_prompt, v_prompt, jnp.stack(lp),
            jnp.stack(cp), jnp.stack(ksm), jnp.stack(vsm), jnp.stack(lsm), jnp.stack(csm))
```

```python
import functools

import jax
import jax.numpy as jnp
from jax import lax
from jax.experimental import pallas as pl
from jax.experimental.pallas import tpu as pltpu

HEAD_DIM = 64
CONV_W = 3
RMS_EPS = 1e-6
LANES = 128
SUBLANES = 8
BF16_ROWS = 16
LOG2E = 1.4426950408889634
NEG_INF = float("-inf")
PAGES_PER_CHUNK = 16
EXP_ROWS = 128
N_PROMPT_IN_INPUTS = 11

_NT = (((1,), (1,)), ((), ()))


def _bf(x):
    return x.astype(jnp.bfloat16)


def _dot(a, b):
    return jnp.dot(a, b, preferred_element_type=jnp.float32)


def _dot_nt(a, b):
    return lax.dot_general(a, b, _NT, preferred_element_type=jnp.float32)


def _silu(x):
    return x * (1.0 / (1.0 + jnp.exp(-x)))


def _log_sigmoid(z):
    return jnp.minimum(z, 0.0) - jnp.log(1.0 + jnp.exp(-jnp.abs(z)))


def _prefix_sum_lanes(x):
    n = x.shape[-1]
    lane = lax.broadcasted_iota(jnp.int32, x.shape, x.ndim - 1)
    sh = 1
    while sh < n:
        x = x + jnp.where(lane >= sh, pltpu.roll(x, sh, x.ndim - 1), 0.0)
        sh *= 2
    return x


def _ada_kernel(c_ref, w_ref, b_ref, o_ref):
    c = c_ref[...]
    o_ref[...] = _dot(_bf(_silu(c)), _bf(w_ref[...])) + b_ref[...]


def _ada(c_all, w_ada, b_ada):
    depth, d, n3 = w_ada.shape
    rows = c_all.shape[0]
    bn = 768
    return pl.pallas_call(
        _ada_kernel,
        grid=(depth, n3 // bn),
        in_specs=[
            pl.BlockSpec((rows, d), lambda l, j: (0, 0)),
            pl.BlockSpec((None, d, bn), lambda l, j: (l, 0, j)),
            pl.BlockSpec((None, 1, bn), lambda l, j: (l, 0, j)),
        ],
        out_specs=pl.BlockSpec((None, rows, bn), lambda l, j: (l, 0, j)),
        out_shape=jax.ShapeDtypeStruct((depth, rows, n3), jnp.float32),
        name="ada_mod",
    )(c_all, w_ada, b_ada.reshape(depth, 1, n3))


def _repack_kernel(wt_ref, o_ref):
    o_ref[...] = _bf(wt_ref[0].T)


def _repack_w_in(w_t, dh, n_skip):
    depth, _, d = w_t.shape
    n_groups = 8

    def src_rows(l, c):
        row = c * dh + jnp.where(c >= n_groups // 2, n_skip, 0)
        return (l, pl.multiple_of(row, SUBLANES), 0)

    return pl.pallas_call(
        _repack_kernel,
        grid=(depth, n_groups),
        in_specs=[pl.BlockSpec((pl.Element(1), pl.Element(dh), pl.Element(d)), src_rows)],
        out_specs=pl.BlockSpec((None, d, dh), lambda l, c: (l, 0, c)),
        out_shape=jax.ShapeDtypeStruct((depth, d, n_groups * dh), jnp.bfloat16),
        name="repack_w_in",
    )(w_t)


def _inproj_math(x, shift, scale, g, w_ref, wf_ref, bf_row, qg, kg, bd):
    dh = qg.shape[-1]
    ms = jnp.mean(x * x, axis=-1, keepdims=True)
    xn = x * lax.rsqrt(ms + RMS_EPS)
    h = _bf(xn * g * (1.0 + scale) + shift)

    def col(c):
        return _dot(h, w_ref[:, c * dh:(c + 1) * dh])

    def head_norm(t, gain):
        t2 = _bf(t * t)
        half = dh // 2
        msq = jnp.concatenate(
            [_dot(t2[:, :half], bd), _dot(t2[:, half:], bd)], axis=1)
        return t * lax.rsqrt(msq + RMS_EPS) * gain

    q = head_norm(col(0), qg)
    k = head_norm(col(1), kg)
    v = col(2)
    sza = _silu(col(3))
    fl = _dot(h, wf_ref[...]) + bf_row
    logf = _log_sigmoid(fl)
    bc = col(4)
    cu = col(5) * col(6)
    zc = col(7)
    return q, k, v, sza, logf, bc, cu, zc


def _prompt_in_kernel(*refs, d_model, n_heads, tk, n_alias):
    (x_ref, mod_ref, g_ref, w_ref, wf_ref, bf_ref, qg_ref, kg_ref,
     cw_ref, cb_ref, bd_ref) = refs[:N_PROMPT_IN_INPUTS]
    (qb_ref, kb_ref, vtb_ref, kt_ref, vt_ref, lft_ref, sza_ref, oc_ref,
     hist_ref, carry_ref) = refs[N_PROMPT_IN_INPUTS + n_alias:]
    i = pl.program_id(1)
    tm = x_ref.shape[0]
    mod = mod_ref[...]
    shift = mod[:, :d_model]
    scale = mod[:, d_model:2 * d_model]
    q, k, v, sza, logf, bc, cu, zc = _inproj_math(
        x_ref[...], shift, scale, g_ref[...], w_ref, wf_ref, bf_ref[...],
        qg_ref[...], kg_ref[...], bd_ref[...])

    qb_ref[...] = _bf(q * (LOG2E * HEAD_DIM ** -0.5))
    kb_ref[...] = _bf(k)
    k_t = k.T
    v_t = v.T
    if n_alias:
        kt_ref[...] = k_t
        vt_ref[...] = v_t
    else:
        for slot in range(kt_ref.shape[0]):
            kt_ref[slot] = k_t
            vt_ref[slot] = v_t
    for c in range(tm // tk):
        vtb_ref[c] = _bf(v_t[:, c * tk:(c + 1) * tk])
    sza_ref[...] = sza
    lft_ref[...] = logf.T[:n_heads, :]

    @pl.when(i == 0)
    def _():
        carry_ref[...] = jnp.zeros_like(carry_ref)

    prev = carry_ref[...]
    row = lax.broadcasted_iota(jnp.int32, cu.shape, 0)
    s1 = pltpu.roll(cu, 1, 0)
    s1 = jnp.where(row == 0, prev[7:8, :], s1)
    s2 = pltpu.roll(cu, 2, 0)
    s2 = jnp.where(row == 0, prev[6:7, :], jnp.where(row == 1, prev[7:8, :], s2))
    cw = cw_ref[...]
    conv = cb_ref[...] + s2 * cw[0:1, :] + s1 * cw[1:2, :] + cu * cw[2:3, :]
    oc_ref[...] = _bf(bc * conv * _silu(zc))
    carry_ref[...] = cu[tm - SUBLANES:, :]

    @pl.when(i == pl.num_programs(1) - 1)
    def _():
        hist_ref[...] = cu[tm - (CONV_W - 1):, :]


def _prompt_in(l, x, mods, norm_g, w_main, w_f, b_f, qg, kg, conv_w, conv_b, bd, tm, tk, kv_prev):
    depth = w_main.shape[0]
    b, s, d = x.shape
    dh = qg.shape[-1]
    n_heads = dh // HEAD_DIM
    nk = s // tk
    grid = (b, s // tm)
    row_spec = lambda width: pl.BlockSpec((None, tm, width), lambda bi, i: (bi, i, 0))
    col_spec = lambda height: pl.BlockSpec((None, height, tm), lambda bi, i: (bi, 0, i))
    if kv_prev:
        stacked_spec = pl.BlockSpec((None, None, dh, tm), lambda bi, i: (l, bi, 0, i))
    else:
        stacked_spec = pl.BlockSpec((depth, None, dh, tm), lambda bi, i: (0, bi, 0, i))
    par = lambda shape: pl.BlockSpec((None,) + shape, lambda bi, i: (l,) + (0,) * len(shape))
    n_alias = len(kv_prev)
    kernel = functools.partial(_prompt_in_kernel, d_model=d, n_heads=n_heads, tk=tk, n_alias=n_alias)
    return pl.pallas_call(
        kernel,
        grid=grid,
        input_output_aliases={N_PROMPT_IN_INPUTS + a: 3 + a for a in range(n_alias)},
        in_specs=[
            row_spec(d),
            pl.BlockSpec((None, None, 1, 3 * d), lambda bi, i: (l, bi, 0, 0)),
            par((1, d)),
            pl.BlockSpec((None, d, 8 * dh), lambda bi, i: (l, 0, 0),
                         pipeline_mode=pl.Buffered(1)),
            par((d, LANES)),
            par((1, LANES)),
            par((1, dh)),
            par((1, dh)),
            par((CONV_W, dh)),
            par((1, dh)),
            pl.BlockSpec(bd.shape, lambda bi, i: (0, 0)),
        ] + [pl.BlockSpec(memory_space=pl.ANY)] * n_alias,
        out_specs=[
            row_spec(dh),
            row_spec(dh),
            pl.BlockSpec((None, tm // tk, dh, tk), lambda bi, i: (bi, i, 0, 0)),
            stacked_spec,
            stacked_spec,
            col_spec(n_heads),
            row_spec(dh),
            row_spec(dh),
            pl.BlockSpec((None, CONV_W - 1, dh), lambda bi, i: (bi, 0, 0)),
        ],
        out_shape=[
            jax.ShapeDtypeStruct((b, s, dh), jnp.bfloat16),
            jax.ShapeDtypeStruct((b, s, dh), jnp.bfloat16),
            jax.ShapeDtypeStruct((b, nk, dh, tk), jnp.bfloat16),
            jax.ShapeDtypeStruct((depth, b, dh, s), jnp.float32),
            jax.ShapeDtypeStruct((depth, b, dh, s), jnp.float32),
            jax.ShapeDtypeStruct((b, n_heads, s), jnp.float32),
            jax.ShapeDtypeStruct((b, s, dh), jnp.float32),
            jax.ShapeDtypeStruct((b, s, dh), jnp.bfloat16),
            jax.ShapeDtypeStruct((b, CONV_W - 1, dh), jnp.float32),
        ],
        scratch_shapes=[pltpu.VMEM((SUBLANES, dh), jnp.float32)],
        compiler_params=pltpu.CompilerParams(
            dimension_semantics=("arbitrary", "arbitrary"),
            vmem_limit_bytes=56 * 1024 * 1024),
        name="prompt_in",
    )(x, mods, norm_g, w_main, w_f, b_f, qg, kg, conv_w, conv_b, bd, *kv_prev)


def _cumsum_kernel(lf_ref, ka_ref):
    n_heads, s = lf_ref.shape
    x = -_prefix_sum_lanes(lf_ref[...] * LOG2E)
    hi = _bf(x).astype(jnp.float32)
    r = x - hi
    mid = _bf(r).astype(jnp.float32)
    lo = r - mid
    pad = jnp.zeros((LANES - 3 * n_heads, s), jnp.float32)
    ka_ref[...] = _bf(jnp.concatenate([hi, mid, lo, pad], axis=0).T)


def _cumsum(lf_t):
    b, h, s = lf_t.shape
    return pl.pallas_call(
        _cumsum_kernel,
        grid=(b,),
        in_specs=[pl.BlockSpec((None, h, s), lambda bi: (bi, 0, 0))],
        out_specs=pl.BlockSpec((None, s, LANES), lambda bi: (bi, 0, 0)),
        out_shape=jax.ShapeDtypeStruct((b, s, LANES), jnp.bfloat16),
        name="forget_cumsum",
    )(lf_t)


def _attn_kernel(pt_ref, q_ref, k_ref, ka_ref, vt_ref, sza_ref,
                 qs_ref, kn_ref, vn_ref, lfn_ref, ck_hbm, cv_hbm, clf_hbm,
                 o_ref, os_ref, s_scr, tmax_scr, m_scr, acc_scr, *sample_scratch,
                 tq, tk, n_heads, n_chunks, sample_params):
    nq = q_ref.shape[0] // tq
    ones_rows = jnp.ones((BF16_ROWS, tk), jnp.bfloat16)
    lane = lax.broadcasted_iota(jnp.int32, (tq, LANES), 1)
    bias_ones = []
    for hh in range(2):
        head = 2 * pl.program_id(1) + hh
        mine = (lane < 3 * n_heads) & (lax.rem(lane, n_heads) == head)
        bias_ones.append(_bf(jnp.where(mine, 1.0, 0.0)))

    def q_rows(qb):
        return pl.ds(pl.multiple_of(qb * tq, tq), tq)

    def q_operands(qb):
        q = q_ref[q_rows(qb), :]
        zero = jnp.zeros_like(q)
        q_split = (jnp.where(lane < HEAD_DIM, q, zero), jnp.where(lane >= HEAD_DIM, q, zero))
        return [jnp.concatenate([q_split[hh], bias_ones[hh]], axis=1) for hh in range(2)]

    def qk(q_heads, j):
        rows = pl.ds(pl.multiple_of(j * tk, tk), tk)
        k_t = jnp.concatenate([k_ref[rows, :], ka_ref[rows, :]], axis=1)
        return [_dot_nt(k_t, q_heads[hh]) for hh in range(2)]

    def scores_to_scratch(s_pair, masked, slot):
        for hh in range(2):
            s_t = s_pair[hh]
            if masked:
                kpos = lax.broadcasted_iota(jnp.int32, s_t.shape, 0)
                qpos = lax.broadcasted_iota(jnp.int32, s_t.shape, 1)
                s_t = jnp.where(kpos <= qpos, s_t, NEG_INF)
            s_scr[slot, hh] = s_t
            col_max = jnp.max(s_t, axis=0, keepdims=True)
            tmax_scr[slot, hh] = jnp.broadcast_to(col_max, (SUBLANES, tq))

    def rep(stat, rows):
        return jnp.broadcast_to(stat[None], (rows // SUBLANES, SUBLANES, tq)).reshape(rows, tq)

    def exps(slot):
        out = []
        for hh in range(2):
            m_old = m_scr[hh]
            m_new = jnp.maximum(m_old, tmax_scr[slot, hh])
            alpha = jnp.exp2(m_old - m_new)
            m_scr[hh] = m_new
            m_rep = rep(m_new, EXP_ROWS)
            p = jnp.concatenate(
                [_bf(jnp.exp2(s_scr[slot, hh, c * EXP_ROWS:(c + 1) * EXP_ROWS, :] - m_rep))
                 for c in range(tk // EXP_ROWS)], axis=0)
            out.append((alpha, p))
        return out

    def pv(j, ap):
        out = []
        for hh in range(2):
            alpha, p = ap[hh]
            v_t = jnp.concatenate(
                [vt_ref[j, hh * HEAD_DIM:(hh + 1) * HEAD_DIM, :], ones_rows], axis=0)
            out.append((alpha, _dot(v_t, p)))
        return out

    def accumulate(av):
        for hh in range(2):
            alpha, pv_t = av[hh]
            acc_scr[hh] = rep(alpha, pv_t.shape[0]) * acc_scr[hh] + pv_t

    def reset_stats():
        m_scr[...] = jnp.full_like(m_scr, NEG_INF)
        acc_scr[...] = jnp.zeros_like(acc_scr)

    scores_to_scratch(qk(q_operands(0), 0), True, 0)
    reset_stats()

    def q_block(qi, carry):
        chunk = (pl.program_id(0) * pl.num_programs(1) + pl.program_id(1)) * nq + qi

        @pl.when(chunk < n_chunks)
        def _():
            _sample_chunk(chunk, n_chunks, pt_ref, qs_ref, kn_ref, vn_ref, lfn_ref,
                          ck_hbm, cv_hbm, clf_hbm, os_ref, *sample_scratch, **sample_params)

        q_heads = q_operands(qi)

        def step(j_next, slot_next):
            ap = exps(1 - slot_next)
            s_next = qk(q_heads, j_next)
            av = pv(j_next - 1, ap)
            scores_to_scratch(s_next, False, slot_next)
            accumulate(av)

        n_plain = jnp.maximum(qi - 1, 0)

        def body(jj, c):
            step(2 * jj + 1, 1)
            step(2 * jj + 2, 0)
            return c

        lax.fori_loop(0, n_plain // 2, body, 0)

        @pl.when(lax.rem(n_plain, 2) == 1)
        def _():
            step(qi - 1, 1)

        def close(slot, with_diagonal):
            if with_diagonal:
                ap = exps(1 - slot)
                s_diag = qk(q_heads, qi)
            s_first = qk(q_operands(jnp.minimum(qi + 1, nq - 1)), 0)
            if with_diagonal:
                av = pv(qi - 1, ap)
                scores_to_scratch(s_diag, True, slot)
                accumulate(av)
            accumulate(pv(qi, exps(slot)))
            o_t = jnp.concatenate(
                [acc_scr[hh, :HEAD_DIM, :] / acc_scr[hh, HEAD_DIM:HEAD_DIM + 1, :]
                 for hh in range(2)], axis=0)
            o_ref[q_rows(qi), :] = _bf(o_t.T * sza_ref[q_rows(qi), :])
            scores_to_scratch(s_first, False, 0)
            reset_stats()

        @pl.when(qi == 0)
        def _():
            close(0, False)

        for parity in range(2):
            @pl.when((qi > 0) & (lax.rem(qi, 2) == parity))
            def _():
                close(parity, True)

        return carry

    lax.fori_loop(0, nq, q_block, 0)


def _attn(qb, kb, ka, vtb, sza, tq, tk, layer_off, page_table, qs, kn, vn, lfn, ck, cv, clf, n_seq):
    b, s, dh = qb.shape
    npair = dh // LANES
    nk = s // tk
    n_heads = dh // HEAD_DIM
    r = qs.shape[0]
    n_pages = page_table.shape[1]
    cpp = min(PAGES_PER_CHUNK, n_pages)
    n_chunks = n_seq * (n_pages // cpp)
    assert n_chunks <= b * npair * (s // tq), "one sample chunk per prompt query block"
    kernel = functools.partial(
        _attn_kernel, tq=tq, tk=tk, n_heads=n_heads, n_chunks=n_chunks,
        sample_params=dict(layer_off=layer_off, n_pages=n_pages, cpp=cpp))
    seq_spec = pl.BlockSpec((None, s, LANES), lambda bi, hp, pt: (bi, 0, hp))
    full = lambda shape: pl.BlockSpec(shape, lambda bi, hp, pt: (0,) * len(shape))
    anyspec = pl.BlockSpec(memory_space=pl.ANY)
    grid_spec = pltpu.PrefetchScalarGridSpec(
        num_scalar_prefetch=1,
        grid=(b, npair),
        in_specs=[
            seq_spec,
            seq_spec,
            pl.BlockSpec((None, s, LANES), lambda bi, hp, pt: (bi, 0, 0)),
            pl.BlockSpec((None, nk, LANES, tk), lambda bi, hp, pt: (bi, 0, hp, 0)),
            seq_spec,
            full((r, dh)), full((r, dh)), full((r, dh)), full((r, LANES)),
            anyspec, anyspec, anyspec,
        ],
        out_specs=[seq_spec, full((r, dh))],
        scratch_shapes=[
            pltpu.VMEM((2, 2, tk, tq), jnp.float32),
            pltpu.VMEM((2, 2, SUBLANES, tq), jnp.float32),
            pltpu.VMEM((2, SUBLANES, tq), jnp.float32),
            pltpu.VMEM((2, HEAD_DIM + BF16_ROWS, tq), jnp.float32),
        ] + _sample_scratch(n_heads, dh, cpp, ck.shape[-1]),
    )
    return pl.pallas_call(
        kernel,
        grid_spec=grid_spec,
        out_shape=[jax.ShapeDtypeStruct((b, s, dh), jnp.bfloat16),
                   jax.ShapeDtypeStruct((r, dh), jnp.float32)],
        compiler_params=pltpu.CompilerParams(
            dimension_semantics=("arbitrary", "arbitrary"),
            vmem_limit_bytes=56 * 1024 * 1024),
        name="attn",
    )(page_table.reshape(-1), qb, kb, ka, vtb, sza, qs, kn, vn, lfn, ck, cv, clf)


def _out_kernel(x_ref, oa_ref, oc_ref, w_ref, gate_ref, y_ref):
    dh = oa_ref.shape[-1]
    y = _dot(oa_ref[...], w_ref[:dh, :]) + _dot(oc_ref[...], w_ref[dh:, :])
    y_ref[...] = x_ref[...] + gate_ref[...] * y


def _prompt_out(l, x, oa, oc, w_out, mods, tm):
    b, s, d = x.shape
    dh = oa.shape[-1]
    row_spec = lambda width: pl.BlockSpec((None, tm, width), lambda bi, i: (bi, i, 0))
    return pl.pallas_call(
        _out_kernel,
        grid=(b, s // tm),
        in_specs=[
            row_spec(d), row_spec(dh), row_spec(dh),
            pl.BlockSpec((None, 2 * dh, d), lambda bi, i: (l, 0, 0)),
            pl.BlockSpec((None, None, 1, d), lambda bi, i: (l, bi, 0, 2)),
        ],
        out_specs=row_spec(d),
        out_shape=jax.ShapeDtypeStruct((b, s, d), jnp.float32),
        compiler_params=pltpu.CompilerParams(
            dimension_semantics=("arbitrary", "arbitrary"),
            vmem_limit_bytes=48 * 1024 * 1024),
        name="prompt_out",
    )(x, oa, oc, w_out, mods)


def _sample_in_kernel(x_ref, mod_ref, st_ref, g_ref, w_ref, wf_ref, bf_ref, qg_ref, kg_ref,
                      cw_ref, cb_ref, bd_ref,
                      q_ref, k_ref, v_ref, lf_ref, sza_ref, oc_ref, hist_ref, *, d_model):
    mod = mod_ref[...]
    shift = mod[:, :d_model]
    scale = mod[:, d_model:2 * d_model]
    q, k, v, sza, logf, bc, cu, zc = _inproj_math(
        x_ref[...], shift, scale, g_ref[...], w_ref, wf_ref, bf_ref[...],
        qg_ref[...], kg_ref[...], bd_ref[...])
    dh = q.shape[-1]
    q_ref[...] = q * (LOG2E * HEAD_DIM ** -0.5)
    k_ref[...] = k
    v_ref[...] = v
    lf_ref[...] = logf
    sza_ref[...] = sza
    st = st_ref[...]
    h0 = st[:, :dh]
    h1 = st[:, dh:]
    cw = cw_ref[...]
    conv = cb_ref[...] + h0 * cw[0:1, :] + h1 * cw[1:2, :] + cu * cw[2:3, :]
    oc_ref[...] = _bf(bc * conv * _silu(zc))
    hist_ref[...] = jnp.concatenate([h1, cu], axis=1)


def _sample_in(l, x, mod_s, st, norm_g, w_main, w_f, b_f, qg, kg, conv_w, conv_b, bd):
    r, d = x.shape
    dh = qg.shape[-1]
    full = lambda shape: pl.BlockSpec(shape, lambda i: (0,) * len(shape))
    par = lambda shape: pl.BlockSpec((None,) + shape, lambda i: (l,) + (0,) * len(shape))
    kernel = functools.partial(_sample_in_kernel, d_model=d)
    f32 = jnp.float32
    return pl.pallas_call(
        kernel,
        grid=(1,),
        in_specs=[
            full((r, d)), par((r, 3 * d)), par((r, 2 * dh)), par((1, d)),
            par((d, 8 * dh)), par((d, LANES)), par((1, LANES)), par((1, dh)), par((1, dh)),
            par((CONV_W, dh)), par((1, dh)), full(bd.shape),
        ],
        out_specs=[
            full((r, dh)), full((r, dh)), full((r, dh)), full((r, LANES)), full((r, dh)),
            full((r, dh)), full((r, 2 * dh)),
        ],
        out_shape=[
            jax.ShapeDtypeStruct((r, dh), f32),
            jax.ShapeDtypeStruct((r, dh), f32),
            jax.ShapeDtypeStruct((r, dh), f32),
            jax.ShapeDtypeStruct((r, LANES), f32),
            jax.ShapeDtypeStruct((r, dh), f32),
            jax.ShapeDtypeStruct((r, dh), jnp.bfloat16),
            jax.ShapeDtypeStruct((r, 2 * dh), f32),
        ],
        compiler_params=pltpu.CompilerParams(vmem_limit_bytes=56 * 1024 * 1024),
        name="sample_in",
    )(x, mod_s, st, norm_g, w_main, w_f, b_f, qg, kg, conv_w, conv_b, bd)


def _sample_chunk(g, nsteps, pt_ref, q_ref, kn_ref, vn_ref, lfn_ref, k_hbm, v_hbm, lf_hbm,
                  o_ref, kbuf, vbuf, lfbuf, sems, acc_ref, m_ref, l_ref, c_ref,
                  *, layer_off, n_pages, cpp):
    n_heads, hd, n_tok = kbuf.shape[1], kbuf.shape[2], kbuf.shape[3]
    dh = n_heads * hd
    page = n_tok // cpp
    nch = n_pages // cpp
    slot = lax.rem(g, 2)
    seq = g // nch
    ch = lax.rem(g, nch)

    def copies(step, sl):
        sq = step // nch
        c0 = lax.rem(step, nch) * cpp
        out = []
        for p in range(cpp):
            pid = pt_ref[sq * n_pages + c0 + p] + layer_off
            lanes = pl.ds(p * page, page)
            out.append(pltpu.make_async_copy(k_hbm.at[pid], kbuf.at[sl, :, :, lanes], sems.at[sl, 0]))
            out.append(pltpu.make_async_copy(v_hbm.at[pid], vbuf.at[sl, :, :, lanes], sems.at[sl, 1]))
            out.append(pltpu.make_async_copy(lf_hbm.at[pid], lfbuf.at[sl, :, lanes], sems.at[sl, 2]))
        return out

    @pl.when(g == 0)
    def _():
        o_ref[...] = jnp.zeros_like(o_ref)
        for cp in copies(0, 0):
            cp.start()

    @pl.when(g + 1 < nsteps)
    def _():
        for cp in copies(g + 1, 1 - slot):
            cp.start()

    for cp in copies(g, slot):
        cp.wait()

    @pl.when(ch == 0)
    def _():
        acc_ref[...] = jnp.zeros_like(acc_ref)
        m_ref[...] = jnp.full_like(m_ref, NEG_INF)
        l_ref[...] = jnp.zeros_like(l_ref)
        c_ref[...] = jnp.zeros_like(c_ref)

    qrow = q_ref[pl.ds(seq, 1), :]
    sub = lax.broadcasted_iota(jnp.int32, (BF16_ROWS, dh), 0)
    lane_head = lax.broadcasted_iota(jnp.int32, (BF16_ROWS, dh), 1) // hd
    diag = sub == lane_head
    qbd = jnp.where(diag, qrow, 0.0)

    kc = _bf(kbuf[slot].reshape(dh, n_tok))
    s = _dot(_bf(qbd), kc)[:n_heads]
    cum = _prefix_sum_lanes(lfbuf[slot] * LOG2E) + c_ref[...]
    c_ref[...] = cum[:, n_tok - 1:n_tok]
    logit = s - cum
    m_old = m_ref[...]
    m_new = jnp.maximum(m_old, jnp.max(logit, axis=1, keepdims=True))
    alpha = jnp.exp2(m_old - m_new)
    p_t = jnp.exp2(logit - m_new)
    l_ref[...] = alpha * l_ref[...] + jnp.sum(p_t, axis=1, keepdims=True)
    m_ref[...] = m_new
    p16 = _bf(jnp.concatenate([p_t, jnp.zeros_like(p_t)], axis=0))
    vc = _bf(vbuf[slot].reshape(dh, n_tok))
    acc_ref[...] = alpha * acc_ref[...] + _dot_nt(p16, vc)[:n_heads]

    @pl.when(ch == nch - 1)
    def _():
        krow = kn_ref[pl.ds(seq, 1), :]
        vrow = vn_ref[pl.ds(seq, 1), :]
        lfrow = lfn_ref[pl.ds(seq, 1), :]
        sub8 = lax.broadcasted_iota(jnp.int32, (n_heads, LANES), 0)
        lane8 = lax.broadcasted_iota(jnp.int32, (n_heads, LANES), 1)
        lfcol = jnp.sum(jnp.where(sub8 == lane8, lfrow, 0.0), axis=1, keepdims=True) * LOG2E
        s_self = jnp.sum(qbd[:n_heads] * krow, axis=1, keepdims=True) - (c_ref[...] + lfcol)
        m_prev = m_ref[...]
        m_fin = jnp.maximum(m_prev, s_self)
        a_fin = jnp.exp2(m_prev - m_fin)
        p_self = jnp.exp2(s_self - m_fin)
        l_fin = a_fin * l_ref[...] + p_self
        o = (a_fin * acc_ref[...] + p_self * vrow) / l_fin
        o_ref[pl.ds(seq, 1), :] = jnp.sum(jnp.where(diag[:n_heads], o, 0.0), axis=0, keepdims=True)


def _sample_scratch(n_heads, dh, cpp, page):
    return [
        pltpu.VMEM((2, n_heads, HEAD_DIM, cpp * page), jnp.float32),
        pltpu.VMEM((2, n_heads, HEAD_DIM, cpp * page), jnp.float32),
        pltpu.VMEM((2, n_heads, cpp * page), jnp.float32),
        pltpu.SemaphoreType.DMA((2, 3)),
        pltpu.VMEM((n_heads, dh), jnp.float32),
        pltpu.VMEM((n_heads, 1), jnp.float32),
        pltpu.VMEM((n_heads, 1), jnp.float32),
        pltpu.VMEM((n_heads, 1), jnp.float32),
    ]


def _sample_out_kernel(x_ref, o_ref, sza_ref, oc_ref, w_ref, mod_ref, y_ref, *, d_model):
    dh = oc_ref.shape[-1]
    oa = _bf(o_ref[...] * sza_ref[...])
    y = _dot(oa, w_ref[:dh, :]) + _dot(oc_ref[...], w_ref[dh:, :])
    gate = mod_ref[:, 2 * d_model:]
    y_ref[...] = x_ref[...] + gate * y


def _sample_out(l, x, o, sza, oc, w_out, mod_s):
    r, d = x.shape
    dh = oc.shape[-1]
    full = lambda shape: pl.BlockSpec(shape, lambda i: (0,) * len(shape))
    par = lambda shape: pl.BlockSpec((None,) + shape, lambda i: (l,) + (0,) * len(shape))
    return pl.pallas_call(
        functools.partial(_sample_out_kernel, d_model=d),
        grid=(1,),
        in_specs=[full((r, d)), full((r, dh)), full((r, dh)), full((r, dh)),
                  par((2 * dh, d)), par((r, 3 * d))],
        out_specs=full((r, d)),
        out_shape=jax.ShapeDtypeStruct((r, d), jnp.float32),
        name="sample_out",
    )(x, o, sza, oc, w_out, mod_s)


def kernel(x_prompt, x_sample, cache_k, cache_v, cache_logf, state_conv, page_table, c_prompt, c_sample,
           w_in, b_forget, w_out, norm_g, q_gain, k_gain, conv_w, conv_b, w_ada, b_ada):
    depth, d, _ = w_in.shape
    b, s, _ = x_prompt.shape
    n_seq = x_sample.shape[0]
    n_heads = b_forget.shape[-1]
    dh = n_heads * HEAD_DIM
    n_phys, page = cache_k.shape[1], cache_k.shape[2]
    tm = min(512, s)
    tq = tk = min(512, s)
    r_pad = -(-n_seq // SUBLANES) * SUBLANES

    f0 = 4 * dh
    w_t = jnp.swapaxes(w_in, 1, 2)
    w_main = _repack_w_in(w_t, dh, n_heads)
    w_f = _bf(jnp.pad(w_in[:, :, f0:f0 + n_heads], ((0, 0), (0, 0), (0, LANES - n_heads))))
    b_f = jnp.pad(b_forget, ((0, 0), (0, LANES - n_heads))).reshape(depth, 1, LANES)
    w_out_b = _bf(w_out)
    qg = jnp.tile(q_gain, (1, n_heads)).reshape(depth, 1, dh)
    kg = jnp.tile(k_gain, (1, n_heads)).reshape(depth, 1, dh)
    g3 = norm_g.reshape(depth, 1, d)
    cb3 = conv_b.reshape(depth, 1, dh)
    half = dh // 2
    hid = jnp.arange(half) // HEAD_DIM
    bd = _bf(jnp.where(hid[:, None] == hid[None, :], 1.0 / HEAD_DIM, 0.0))

    n_prompt = c_prompt.shape[0]
    c_all = jnp.concatenate([c_prompt, c_sample], axis=0)
    rows = c_all.shape[0]
    rows_pad = -(-rows // SUBLANES) * SUBLANES
    c_all = jnp.pad(c_all, ((0, rows_pad - rows), (0, 0)))
    mods = _ada(c_all, w_ada, b_ada)
    mods4 = mods.reshape(depth, rows_pad, 1, 3 * d)
    mod_s = jnp.pad(mods[:, n_prompt:n_prompt + n_seq, :], ((0, 0), (0, r_pad - n_seq), (0, 0)))

    ck = jnp.transpose(cache_k, (0, 1, 3, 4, 2)).reshape(depth * n_phys, n_heads, HEAD_DIM, page)
    cv = jnp.transpose(cache_v, (0, 1, 3, 4, 2)).reshape(depth * n_phys, n_heads, HEAD_DIM, page)
    clf = jnp.transpose(cache_logf, (0, 1, 3, 2)).reshape(depth * n_phys, n_heads, page)
    st_all = jnp.pad(state_conv.reshape(depth, n_seq, (CONV_W - 1) * dh),
                     ((0, 0), (0, r_pad - n_seq), (0, 0)))

    yp = x_prompt
    ys = jnp.pad(x_sample.reshape(n_seq, d), ((0, r_pad - n_seq), (0, 0)))
    lp, cp = [], []
    ksm, vsm, lsm, csm = [], [], [], []
    kv_all = ()
    for l in range(depth):
        qb, kb, vtb, kt_all, vt_all, lft, sza, oc, hist = _prompt_in(
            l, yp, mods4, g3, w_main, w_f, b_f, qg, kg, conv_w, cb3, bd, tm, tk, kv_all)
        kv_all = (kt_all, vt_all)
        ka = _cumsum(lft)
        q_s, k_s, v_s, lf_s, sza_s, oc_s, hist_s = _sample_in(
            l, ys, mod_s, st_all, g3, w_main, w_f, b_f, qg, kg, conv_w, cb3, bd)
        oa, o_s = _attn(qb, kb, ka, vtb, sza, tq, tk,
                        l * n_phys, page_table, q_s, k_s, v_s, lf_s, ck, cv, clf, n_seq)
        yp = _prompt_out(l, yp, oa, oc, w_out_b, mods4, min(1024, s))
        ys = _sample_out(l, ys, o_s, sza_s, oc_s, w_out_b, mod_s)
        lp.append(jnp.transpose(lft, (0, 2, 1)))
        cp.append(hist)
        ksm.append(k_s[:n_seq].reshape(n_seq, 1, n_heads, HEAD_DIM))
        vsm.append(v_s[:n_seq].reshape(n_seq, 1, n_heads, HEAD_DIM))
        lsm.append(lf_s[:n_seq, :n_heads].reshape(n_seq, 1, n_heads))
        csm.append(hist_s[:n_seq].reshape(n_seq, CONV_W - 1, dh))
    k_prompt, v_prompt = (
        jnp.transpose(a.reshape(depth, b, n_heads, HEAD_DIM, s), (0, 1, 4, 2, 3)) for a in kv_all)

    return (yp, ys[:n_seq].reshape(n_seq, 1, d), k_prompt, v_prompt, jnp.stack(lp),
            jnp.stack(cp), jnp.stack(ksm), jnp.stack(vsm), jnp.stack(lsm), jnp.stack(csm))
```

```python
import functools

import jax
import jax.numpy as jnp
from jax import lax
from jax.experimental import pallas as pl
from jax.experimental.pallas import tpu as pltpu

HEAD_DIM = 64
CONV_W = 3
RMS_EPS = 1e-6
LANES = 128
SUBLANES = 8
BF16_ROWS = 16
LOG2E = 1.4426950408889634
NEG_INF = float("-inf")
PAGES_PER_CHUNK = 16
EXP_ROWS = 128
N_PROMPT_IN_INPUTS = 11

_NT = (((1,), (1,)), ((), ()))


def _bf(x):
    return x.astype(jnp.bfloat16)


def _dot(a, b):
    return jnp.dot(a, b, preferred_element_type=jnp.float32)


def _dot_nt(a, b):
    return lax.dot_general(a, b, _NT, preferred_element_type=jnp.float32)


def _silu(x):
    return x * (1.0 / (1.0 + jnp.exp(-x)))


def _log_sigmoid(z):
    return jnp.minimum(z, 0.0) - jnp.log(1.0 + jnp.exp(-jnp.abs(z)))


def _prefix_sum_lanes(x):
    n = x.shape[-1]
    lane = lax.broadcasted_iota(jnp.int32, x.shape, x.ndim - 1)
    sh = 1
    while sh < n:
        x = x + jnp.where(lane >= sh, pltpu.roll(x, sh, x.ndim - 1), 0.0)
        sh *= 2
    return x


def _ada_kernel(c_ref, w_ref, b_ref, o_ref):
    c = c_ref[...]
    o_ref[...] = _dot(_bf(_silu(c)), _bf(w_ref[...])) + b_ref[...]


def _ada(c_all, w_ada, b_ada):
    depth, d, n3 = w_ada.shape
    rows = c_all.shape[0]
    bn = 768
    return pl.pallas_call(
        _ada_kernel,
        grid=(depth, n3 // bn),
        in_specs=[
            pl.BlockSpec((rows, d), lambda l, j: (0, 0)),
            pl.BlockSpec((None, d, bn), lambda l, j: (l, 0, j)),
            pl.BlockSpec((None, 1, bn), lambda l, j: (l, 0, j)),
        ],
        out_specs=pl.BlockSpec((None, rows, bn), lambda l, j: (l, 0, j)),
        out_shape=jax.ShapeDtypeStruct((depth, rows, n3), jnp.float32),
        name="ada_mod",
    )(c_all, w_ada, b_ada.reshape(depth, 1, n3))


def _repack_kernel(wt_ref, o_ref):
    o_ref[...] = _bf(wt_ref[0].T)


def _repack_w_in(w_t, dh, n_skip):
    depth, _, d = w_t.shape
    n_groups = 8

    def src_rows(l, c):
        row = c * dh + jnp.where(c >= n_groups // 2, n_skip, 0)
        return (l, pl.multiple_of(row, SUBLANES), 0)

    return pl.pallas_call(
        _repack_kernel,
        grid=(depth, n_groups),
        in_specs=[pl.BlockSpec((pl.Element(1), pl.Element(dh), pl.Element(d)), src_rows)],
        out_specs=pl.BlockSpec((None, d, dh), lambda l, c: (l, 0, c)),
        out_shape=jax.ShapeDtypeStruct((depth, d, n_groups * dh), jnp.bfloat16),
        name="repack_w_in",
    )(w_t)


def _inproj_math(x, shift, scale, g, w_ref, wf_ref, bf_row, qg, kg, bd):
    dh = qg.shape[-1]
    ms = jnp.mean(x * x, axis=-1, keepdims=True)
    xn = x * lax.rsqrt(ms + RMS_EPS)
    h = _bf(xn * g * (1.0 + scale) + shift)

    def col(c):
        return _dot(h, w_ref[:, c * dh:(c + 1) * dh])

    def head_norm(t, gain):
        t2 = _bf(t * t)
        half = dh // 2
        msq = jnp.concatenate(
            [_dot(t2[:, :half], bd), _dot(t2[:, half:], bd)], axis=1)
        return t * lax.rsqrt(msq + RMS_EPS) * gain

    q = head_norm(col(0), qg)
    k = head_norm(col(1), kg)
    v = col(2)
    sza = _silu(col(3))
    fl = _dot(h, wf_ref[...]) + bf_row
    logf = _log_sigmoid(fl)
    bc = col(4)
    cu = col(5) * col(6)
    zc = col(7)
    return q, k, v, sza, logf, bc, cu, zc


def _prompt_in_kernel(*refs, d_model, n_heads, tk, n_alias):
    (x_ref, mod_ref, g_ref, w_ref, wf_ref, bf_ref, qg_ref, kg_ref,
     cw_ref, cb_ref, bd_ref) = refs[:N_PROMPT_IN_INPUTS]
    (qb_ref, kb_ref, vtb_ref, kt_ref, vt_ref, lft_ref, sza_ref, oc_ref,
     hist_ref, carry_ref) = refs[N_PROMPT_IN_INPUTS + n_alias:]
    i = pl.program_id(1)
    tm = x_ref.shape[0]
    mod = mod_ref[...]
    shift = mod[:, :d_model]
    scale = mod[:, d_model:2 * d_model]
    q, k, v, sza, logf, bc, cu, zc = _inproj_math(
        x_ref[...], shift, scale, g_ref[...], w_ref, wf_ref, bf_ref[...],
        qg_ref[...], kg_ref[...], bd_ref[...])

    qb_ref[...] = _bf(q * (LOG2E * HEAD_DIM ** -0.5))
    kb_ref[...] = _bf(k)
    k_t = k.T
    v_t = v.T
    if n_alias:
        kt_ref[...] = k_t
        vt_ref[...] = v_t
    else:
        for slot in range(kt_ref.shape[0]):
            kt_ref[slot] = k_t
            vt_ref[slot] = v_t
    for c in range(tm // tk):
        vtb_ref[c] = _bf(v_t[:, c * tk:(c + 1) * tk])
    sza_ref[...] = sza
    lft_ref[...] = logf.T[:n_heads, :]

    @pl.when(i == 0)
    def _():
        carry_ref[...] = jnp.zeros_like(carry_ref)

    prev = carry_ref[...]
    row = lax.broadcasted_iota(jnp.int32, cu.shape, 0)
    s1 = pltpu.roll(cu, 1, 0)
    s1 = jnp.where(row == 0, prev[7:8, :], s1)
    s2 = pltpu.roll(cu, 2, 0)
    s2 = jnp.where(row == 0, prev[6:7, :], jnp.where(row == 1, prev[7:8, :], s2))
    cw = cw_ref[...]
    conv = cb_ref[...] + s2 * cw[0:1, :] + s1 * cw[1:2, :] + cu * cw[2:3, :]
    oc_ref[...] = _bf(bc * conv * _silu(zc))
    carry_ref[...] = cu[tm - SUBLANES:, :]

    @pl.when(i == pl.num_programs(1) - 1)
    def _():
        hist_ref[...] = cu[tm - (CONV_W - 1):, :]


def _prompt_in(l, x, mods, norm_g, w_main, w_f, b_f, qg, kg, conv_w, conv_b, bd, tm, tk, kv_prev):
    depth = w_main.shape[0]
    b, s, d = x.shape
    dh = qg.shape[-1]
    n_heads = dh // HEAD_DIM
    nk = s // tk
    grid = (b, s // tm)
    row_spec = lambda width: pl.BlockSpec((None, tm, width), lambda bi, i: (bi, i, 0))
    col_spec = lambda height: pl.BlockSpec((None, height, tm), lambda bi, i: (bi, 0, i))
    if kv_prev:
        stacked_spec = pl.BlockSpec((None, None, dh, tm), lambda bi, i: (l, bi, 0, i))
    else:
        stacked_spec = pl.BlockSpec((depth, None, dh, tm), lambda bi, i: (0, bi, 0, i))
    par = lambda shape: pl.BlockSpec((None,) + shape, lambda bi, i: (l,) + (0,) * len(shape))
    n_alias = len(kv_prev)
    kernel = functools.partial(_prompt_in_kernel, d_model=d, n_heads=n_heads, tk=tk, n_alias=n_alias)
    return pl.pallas_call(
        kernel,
        grid=grid,
        input_output_aliases={N_PROMPT_IN_INPUTS + a: 3 + a for a in range(n_alias)},
        in_specs=[
            row_spec(d),
            pl.BlockSpec((None, None, 1, 3 * d), lambda bi, i: (l, bi, 0, 0)),
            par((1, d)),
            pl.BlockSpec((None, d, 8 * dh), lambda bi, i: (l, 0, 0),
                         pipeline_mode=pl.Buffered(1)),
            par((d, LANES)),
            par((1, LANES)),
            par((1, dh)),
            par((1, dh)),
            par((CONV_W, dh)),
            par((1, dh)),
            pl.BlockSpec(bd.shape, lambda bi, i: (0, 0)),
        ] + [pl.BlockSpec(memory_space=pl.ANY)] * n_alias,
        out_specs=[
            row_spec(dh),
            row_spec(dh),
            pl.BlockSpec((None, tm // tk, dh, tk), lambda bi, i: (bi, i, 0, 0)),
            stacked_spec,
            stacked_spec,
            col_spec(n_heads),
            row_spec(dh),
            row_spec(dh),
            pl.BlockSpec((None, CONV_W - 1, dh), lambda bi, i: (bi, 0, 0)),
        ],
        out_shape=[
            jax.ShapeDtypeStruct((b, s, dh), jnp.bfloat16),
            jax.ShapeDtypeStruct((b, s, dh), jnp.bfloat16),
            jax.ShapeDtypeStruct((b, nk, dh, tk), jnp.bfloat16),
            jax.ShapeDtypeStruct((depth, b, dh, s), jnp.float32),
            jax.ShapeDtypeStruct((depth, b, dh, s), jnp.float32),
            jax.ShapeDtypeStruct((b, n_heads, s), jnp.float32),
            jax.ShapeDtypeStruct((b, s, dh), jnp.float32),
            jax.ShapeDtypeStruct((b, s, dh), jnp.bfloat16),
            jax.ShapeDtypeStruct((b, CONV_W - 1, dh), jnp.float32),
        ],
        scratch_shapes=[pltpu.VMEM((SUBLANES, dh), jnp.float32)],
        compiler_params=pltpu.CompilerParams(
            dimension_semantics=("arbitrary", "arbitrary"),
            vmem_limit_bytes=56 * 1024 * 1024),
        name="prompt_in",
    )(x, mods, norm_g, w_main, w_f, b_f, qg, kg, conv_w, conv_b, bd, *kv_prev)


def _cumsum_kernel(lf_ref, ka_ref):
    n_heads, s = lf_ref.shape
    x = -_prefix_sum_lanes(lf_ref[...] * LOG2E)
    hi = _bf(x).astype(jnp.float32)
    r = x - hi
    mid = _bf(r).astype(jnp.float32)
    lo = r - mid
    pad = jnp.zeros((LANES - 3 * n_heads, s), jnp.float32)
    ka_ref[...] = _bf(jnp.concatenate([hi, mid, lo, pad], axis=0).T)


def _cumsum(lf_t):
    b, h, s = lf_t.shape
    return pl.pallas_call(
        _cumsum_kernel,
        grid=(b,),
        in_specs=[pl.BlockSpec((None, h, s), lambda bi: (bi, 0, 0))],
        out_specs=pl.BlockSpec((None, s, LANES), lambda bi: (bi, 0, 0)),
        out_shape=jax.ShapeDtypeStruct((b, s, LANES), jnp.bfloat16),
        name="forget_cumsum",
    )(lf_t)


def _attn_kernel(pt_ref, q_ref, k_ref, ka_ref, vt_ref, sza_ref,
                 qs_ref, kn_ref, vn_ref, lfn_ref, ck_hbm, cv_hbm, clf_hbm,
                 o_ref, os_ref, s_scr, tmax_scr, m_scr, acc_scr, *sample_scratch,
                 tq, tk, n_heads, n_chunks, every_block_has_chunk, sample_params):
    nq = q_ref.shape[0] // tq
    ones_rows = jnp.ones((BF16_ROWS, tk), jnp.bfloat16)
    lane = lax.broadcasted_iota(jnp.int32, (tq, LANES), 1)
    bias_ones = []
    for hh in range(2):
        head = 2 * pl.program_id(1) + hh
        mine = (lane < 3 * n_heads) & (lax.rem(lane, n_heads) == head)
        bias_ones.append(_bf(jnp.where(mine, 1.0, 0.0)))

    def q_rows(qb):
        return pl.ds(pl.multiple_of(qb * tq, tq), tq)

    def q_operands(qb):
        q = q_ref[q_rows(qb), :]
        zero = jnp.zeros_like(q)
        q_split = (jnp.where(lane < HEAD_DIM, q, zero), jnp.where(lane >= HEAD_DIM, q, zero))
        return [jnp.concatenate([q_split[hh], bias_ones[hh]], axis=1) for hh in range(2)]

    def qk(q_heads, j):
        rows = pl.ds(pl.multiple_of(j * tk, tk), tk)
        k_t = jnp.concatenate([k_ref[rows, :], ka_ref[rows, :]], axis=1)
        return [_dot_nt(k_t, q_heads[hh]) for hh in range(2)]

    def scores_to_scratch(s_pair, masked, slot):
        for hh in range(2):
            s_t = s_pair[hh]
            if masked:
                kpos = lax.broadcasted_iota(jnp.int32, s_t.shape, 0)
                qpos = lax.broadcasted_iota(jnp.int32, s_t.shape, 1)
                s_t = jnp.where(kpos <= qpos, s_t, NEG_INF)
            s_scr[slot, hh] = s_t
            col_max = jnp.max(s_t, axis=0, keepdims=True)
            tmax_scr[slot, hh] = jnp.broadcast_to(col_max, (SUBLANES, tq))

    def rep(stat, rows):
        return jnp.broadcast_to(stat[None], (rows // SUBLANES, SUBLANES, tq)).reshape(rows, tq)

    def exps(slot):
        out = []
        for hh in range(2):
            m_old = m_scr[hh]
            m_new = jnp.maximum(m_old, tmax_scr[slot, hh])
            alpha = jnp.exp2(m_old - m_new)
            m_scr[hh] = m_new
            m_rep = rep(m_new, EXP_ROWS)
            p = jnp.concatenate(
                [_bf(jnp.exp2(s_scr[slot, hh, c * EXP_ROWS:(c + 1) * EXP_ROWS, :] - m_rep))
                 for c in range(tk // EXP_ROWS)], axis=0)
            out.append((alpha, p))
        return out

    def pv(j, ap):
        out = []
        for hh in range(2):
            alpha, p = ap[hh]
            v_t = jnp.concatenate(
                [vt_ref[j, hh * HEAD_DIM:(hh + 1) * HEAD_DIM, :], ones_rows], axis=0)
            out.append((alpha, _dot(v_t, p)))
        return out

    def accumulate(av):
        for hh in range(2):
            alpha, pv_t = av[hh]
            acc_scr[hh] = rep(alpha, pv_t.shape[0]) * acc_scr[hh] + pv_t

    def reset_stats():
        m_scr[...] = jnp.full_like(m_scr, NEG_INF)
        acc_scr[...] = jnp.zeros_like(acc_scr)

    scores_to_scratch(qk(q_operands(0), 0), True, 0)
    reset_stats()

    def q_block(qi, carry):
        chunk = (pl.program_id(0) * pl.num_programs(1) + pl.program_id(1)) * nq + qi

        def sample(stages):
            _sample_chunk(chunk, n_chunks, pt_ref, qs_ref, kn_ref, vn_ref, lfn_ref,
                          ck_hbm, cv_hbm, clf_hbm, os_ref, *sample_scratch,
                          stages=stages, **sample_params)

        if every_block_has_chunk:
            sample(("fetch",))
        else:
            @pl.when(chunk < n_chunks)
            def _():
                sample(("fetch", "fold"))

        q_heads = q_operands(qi)

        def step(j_next, slot_next):
            ap = exps(1 - slot_next)
            s_next = qk(q_heads, j_next)
            av = pv(j_next - 1, ap)
            scores_to_scratch(s_next, False, slot_next)
            accumulate(av)

        n_plain = jnp.maximum(qi - 1, 0)

        def body(jj, c):
            step(2 * jj + 1, 1)
            step(2 * jj + 2, 0)
            return c

        lax.fori_loop(0, n_plain // 2, body, 0)

        @pl.when(lax.rem(n_plain, 2) == 1)
        def _():
            step(qi - 1, 1)

        def close(slot, with_diagonal):
            if with_diagonal:
                ap = exps(1 - slot)
                s_diag = qk(q_heads, qi)
            s_first = qk(q_operands(jnp.minimum(qi + 1, nq - 1)), 0)
            if with_diagonal:
                av = pv(qi - 1, ap)
                scores_to_scratch(s_diag, True, slot)
                accumulate(av)
            accumulate(pv(qi, exps(slot)))
            o_t = jnp.concatenate(
                [acc_scr[hh, :HEAD_DIM, :] / acc_scr[hh, HEAD_DIM:HEAD_DIM + 1, :]
                 for hh in range(2)], axis=0)
            o_ref[q_rows(qi), :] = _bf(o_t.T * sza_ref[q_rows(qi), :])
            scores_to_scratch(s_first, False, 0)
            reset_stats()
            if every_block_has_chunk:
                sample(("fold",))

        @pl.when(qi == 0)
        def _():
            close(0, False)

        for parity in range(2):
            @pl.when((qi > 0) & (lax.rem(qi, 2) == parity))
            def _():
                close(parity, True)

        return carry

    lax.fori_loop(0, nq, q_block, 0)


def _attn(qb, kb, ka, vtb, sza, tq, tk, layer_off, page_table, qs, kn, vn, lfn, ck, cv, clf, n_seq):
    b, s, dh = qb.shape
    npair = dh // LANES
    nk = s // tk
    n_heads = dh // HEAD_DIM
    r = qs.shape[0]
    n_pages = page_table.shape[1]
    cpp = min(PAGES_PER_CHUNK, n_pages)
    n_chunks = n_seq * (n_pages // cpp)
    assert n_chunks <= b * npair * (s // tq), "one sample chunk per prompt query block"
    kernel = functools.partial(
        _attn_kernel, tq=tq, tk=tk, n_heads=n_heads, n_chunks=n_chunks,
        every_block_has_chunk=(n_chunks == b * npair * (s // tq)),
        sample_params=dict(layer_off=layer_off, n_pages=n_pages, cpp=cpp))
    seq_spec = pl.BlockSpec((None, s, LANES), lambda bi, hp, pt: (bi, 0, hp))
    full = lambda shape: pl.BlockSpec(shape, lambda bi, hp, pt: (0,) * len(shape))
    anyspec = pl.BlockSpec(memory_space=pl.ANY)
    grid_spec = pltpu.PrefetchScalarGridSpec(
        num_scalar_prefetch=1,
        grid=(b, npair),
        in_specs=[
            seq_spec,
            seq_spec,
            pl.BlockSpec((None, s, LANES), lambda bi, hp, pt: (bi, 0, 0)),
            pl.BlockSpec((None, nk, LANES, tk), lambda bi, hp, pt: (bi, 0, hp, 0)),
            seq_spec,
            full((r, dh)), full((r, dh)), full((r, dh)), full((r, LANES)),
            anyspec, anyspec, anyspec,
        ],
        out_specs=[seq_spec, full((r, dh))],
        scratch_shapes=[
            pltpu.VMEM((2, 2, tk, tq), jnp.float32),
            pltpu.VMEM((2, 2, SUBLANES, tq), jnp.float32),
            pltpu.VMEM((2, SUBLANES, tq), jnp.float32),
            pltpu.VMEM((2, HEAD_DIM + BF16_ROWS, tq), jnp.float32),
        ] + _sample_scratch(n_heads, dh, cpp, ck.shape[-1]),
    )
    return pl.pallas_call(
        kernel,
        grid_spec=grid_spec,
        out_shape=[jax.ShapeDtypeStruct((b, s, dh), jnp.bfloat16),
                   jax.ShapeDtypeStruct((r, dh), jnp.float32)],
        compiler_params=pltpu.CompilerParams(
            dimension_semantics=("arbitrary", "arbitrary"),
            vmem_limit_bytes=56 * 1024 * 1024),
        name="attn",
    )(page_table.reshape(-1), qb, kb, ka, vtb, sza, qs, kn, vn, lfn, ck, cv, clf)


def _out_kernel(x_ref, oa_ref, oc_ref, w_ref, gate_ref, y_ref):
    dh = oa_ref.shape[-1]
    y = _dot(oa_ref[...], w_ref[:dh, :]) + _dot(oc_ref[...], w_ref[dh:, :])
    y_ref[...] = x_ref[...] + gate_ref[...] * y


def _prompt_out(l, x, oa, oc, w_out, mods, tm):
    b, s, d = x.shape
    dh = oa.shape[-1]
    row_spec = lambda width: pl.BlockSpec((None, tm, width), lambda bi, i: (bi, i, 0))
    return pl.pallas_call(
        _out_kernel,
        grid=(b, s // tm),
        in_specs=[
            row_spec(d), row_spec(dh), row_spec(dh),
            pl.BlockSpec((None, 2 * dh, d), lambda bi, i: (l, 0, 0)),
            pl.BlockSpec((None, None, 1, d), lambda bi, i: (l, bi, 0, 2)),
        ],
        out_specs=row_spec(d),
        out_shape=jax.ShapeDtypeStruct((b, s, d), jnp.float32),
        compiler_params=pltpu.CompilerParams(
            dimension_semantics=("arbitrary", "arbitrary"),
            vmem_limit_bytes=48 * 1024 * 1024),
        name="prompt_out",
    )(x, oa, oc, w_out, mods)


def _sample_in_kernel(x_ref, mod_ref, st_ref, g_ref, w_ref, wf_ref, bf_ref, qg_ref, kg_ref,
                      cw_ref, cb_ref, bd_ref,
                      q_ref, k_ref, v_ref, lf_ref, sza_ref, oc_ref, hist_ref, *, d_model):
    mod = mod_ref[...]
    shift = mod[:, :d_model]
    scale = mod[:, d_model:2 * d_model]
    q, k, v, sza, logf, bc, cu, zc = _inproj_math(
        x_ref[...], shift, scale, g_ref[...], w_ref, wf_ref, bf_ref[...],
        qg_ref[...], kg_ref[...], bd_ref[...])
    dh = q.shape[-1]
    q_ref[...] = q * (LOG2E * HEAD_DIM ** -0.5)
    k_ref[...] = k
    v_ref[...] = v
    lf_ref[...] = logf
    sza_ref[...] = sza
    st = st_ref[...]
    h0 = st[:, :dh]
    h1 = st[:, dh:]
    cw = cw_ref[...]
    conv = cb_ref[...] + h0 * cw[0:1, :] + h1 * cw[1:2, :] + cu * cw[2:3, :]
    oc_ref[...] = _bf(bc * conv * _silu(zc))
    hist_ref[...] = jnp.concatenate([h1, cu], axis=1)


def _sample_in(l, x, mod_s, st, norm_g, w_main, w_f, b_f, qg, kg, conv_w, conv_b, bd):
    r, d = x.shape
    dh = qg.shape[-1]
    full = lambda shape: pl.BlockSpec(shape, lambda i: (0,) * len(shape))
    par = lambda shape: pl.BlockSpec((None,) + shape, lambda i: (l,) + (0,) * len(shape))
    kernel = functools.partial(_sample_in_kernel, d_model=d)
    f32 = jnp.float32
    return pl.pallas_call(
        kernel,
        grid=(1,),
        in_specs=[
            full((r, d)), par((r, 3 * d)), par((r, 2 * dh)), par((1, d)),
            par((d, 8 * dh)), par((d, LANES)), par((1, LANES)), par((1, dh)), par((1, dh)),
            par((CONV_W, dh)), par((1, dh)), full(bd.shape),
        ],
        out_specs=[
            full((r, dh)), full((r, dh)), full((r, dh)), full((r, LANES)), full((r, dh)),
            full((r, dh)), full((r, 2 * dh)),
        ],
        out_shape=[
            jax.ShapeDtypeStruct((r, dh), f32),
            jax.ShapeDtypeStruct((r, dh), f32),
            jax.ShapeDtypeStruct((r, dh), f32),
            jax.ShapeDtypeStruct((r, LANES), f32),
            jax.ShapeDtypeStruct((r, dh), f32),
            jax.ShapeDtypeStruct((r, dh), jnp.bfloat16),
            jax.ShapeDtypeStruct((r, 2 * dh), f32),
        ],
        compiler_params=pltpu.CompilerParams(vmem_limit_bytes=56 * 1024 * 1024),
        name="sample_in",
    )(x, mod_s, st, norm_g, w_main, w_f, b_f, qg, kg, conv_w, conv_b, bd)


def _sample_chunk(g, nsteps, pt_ref, q_ref, kn_ref, vn_ref, lfn_ref, k_hbm, v_hbm, lf_hbm,
                  o_ref, kbuf, vbuf, lfbuf, sems, acc_ref, m_ref, l_ref, c_ref,
                  *, layer_off, n_pages, cpp, stages=("fetch", "fold")):
    n_heads, hd, n_tok = kbuf.shape[1], kbuf.shape[2], kbuf.shape[3]
    dh = n_heads * hd
    page = n_tok // cpp
    nch = n_pages // cpp
    slot = lax.rem(g, 2)
    seq = g // nch
    ch = lax.rem(g, nch)

    def copies(step, sl):
        sq = step // nch
        c0 = lax.rem(step, nch) * cpp
        out = []
        for p in range(cpp):
            pid = pt_ref[sq * n_pages + c0 + p] + layer_off
            lanes = pl.ds(p * page, page)
            out.append(pltpu.make_async_copy(k_hbm.at[pid], kbuf.at[sl, :, :, lanes], sems.at[sl, 0]))
            out.append(pltpu.make_async_copy(v_hbm.at[pid], vbuf.at[sl, :, :, lanes], sems.at[sl, 1]))
            out.append(pltpu.make_async_copy(lf_hbm.at[pid], lfbuf.at[sl, :, lanes], sems.at[sl, 2]))
        return out

    if "fetch" in stages:
        @pl.when(g == 0)
        def _():
            o_ref[...] = jnp.zeros_like(o_ref)
            for cp in copies(0, 0):
                cp.start()

        @pl.when(g + 1 < nsteps)
        def _():
            for cp in copies(g + 1, 1 - slot):
                cp.start()

        for cp in copies(g, slot):
            cp.wait()

        @pl.when(ch == 0)
        def _():
            acc_ref[...] = jnp.zeros_like(acc_ref)
            m_ref[...] = jnp.full_like(m_ref, NEG_INF)
            l_ref[...] = jnp.zeros_like(l_ref)
            c_ref[...] = jnp.zeros_like(c_ref)

    if "fold" not in stages:
        return

    qrow = q_ref[pl.ds(seq, 1), :]
    sub = lax.broadcasted_iota(jnp.int32, (BF16_ROWS, dh), 0)
    lane_head = lax.broadcasted_iota(jnp.int32, (BF16_ROWS, dh), 1) // hd
    diag = sub == lane_head
    qbd = jnp.where(diag, qrow, 0.0)

    kc = _bf(kbuf[slot].reshape(dh, n_tok))
    s = _dot(_bf(qbd), kc)[:n_heads]
    cum = _prefix_sum_lanes(lfbuf[slot] * LOG2E) + c_ref[...]
    c_ref[...] = cum[:, n_tok - 1:n_tok]
    logit = s - cum
    m_old = m_ref[...]
    m_new = jnp.maximum(m_old, jnp.max(logit, axis=1, keepdims=True))
    alpha = jnp.exp2(m_old - m_new)
    p_t = jnp.exp2(logit - m_new)
    l_ref[...] = alpha * l_ref[...] + jnp.sum(p_t, axis=1, keepdims=True)
    m_ref[...] = m_new
    p16 = _bf(jnp.concatenate([p_t, jnp.zeros_like(p_t)], axis=0))
    vc = _bf(vbuf[slot].reshape(dh, n_tok))
    acc_ref[...] = alpha * acc_ref[...] + _dot_nt(p16, vc)[:n_heads]

    @pl.when(ch == nch - 1)
    def _():
        krow = kn_ref[pl.ds(seq, 1), :]
        vrow = vn_ref[pl.ds(seq, 1), :]
        lfrow = lfn_ref[pl.ds(seq, 1), :]
        sub8 = lax.broadcasted_iota(jnp.int32, (n_heads, LANES), 0)
        lane8 = lax.broadcasted_iota(jnp.int32, (n_heads, LANES), 1)
        lfcol = jnp.sum(jnp.where(sub8 == lane8, lfrow, 0.0), axis=1, keepdims=True) * LOG2E
        s_self = jnp.sum(qbd[:n_heads] * krow, axis=1, keepdims=True) - (c_ref[...] + lfcol)
        m_prev = m_ref[...]
        m_fin = jnp.maximum(m_prev, s_self)
        a_fin = jnp.exp2(m_prev - m_fin)
        p_self = jnp.exp2(s_self - m_fin)
        l_fin = a_fin * l_ref[...] + p_self
        o = (a_fin * acc_ref[...] + p_self * vrow) / l_fin
        o_ref[pl.ds(seq, 1), :] = jnp.sum(jnp.where(diag[:n_heads], o, 0.0), axis=0, keepdims=True)


def _sample_scratch(n_heads, dh, cpp, page):
    return [
        pltpu.VMEM((2, n_heads, HEAD_DIM, cpp * page), jnp.float32),
        pltpu.VMEM((2, n_heads, HEAD_DIM, cpp * page), jnp.float32),
        pltpu.VMEM((2, n_heads, cpp * page), jnp.float32),
        pltpu.SemaphoreType.DMA((2, 3)),
        pltpu.VMEM((n_heads, dh), jnp.float32),
        pltpu.VMEM((n_heads, 1), jnp.float32),
        pltpu.VMEM((n_heads, 1), jnp.float32),
        pltpu.VMEM((n_heads, 1), jnp.float32),
    ]


def _sample_out_kernel(x_ref, o_ref, sza_ref, oc_ref, w_ref, mod_ref, y_ref, *, d_model):
    dh = oc_ref.shape[-1]
    oa = _bf(o_ref[...] * sza_ref[...])
    y = _dot(oa, w_ref[:dh, :]) + _dot(oc_ref[...], w_ref[dh:, :])
    gate = mod_ref[:, 2 * d_model:]
    y_ref[...] = x_ref[...] + gate * y


def _sample_out(l, x, o, sza, oc, w_out, mod_s):
    r, d = x.shape
    dh = oc.shape[-1]
    full = lambda shape: pl.BlockSpec(shape, lambda i: (0,) * len(shape))
    par = lambda shape: pl.BlockSpec((None,) + shape, lambda i: (l,) + (0,) * len(shape))
    return pl.pallas_call(
        functools.partial(_sample_out_kernel, d_model=d),
        grid=(1,),
        in_specs=[full((r, d)), full((r, dh)), full((r, dh)), full((r, dh)),
                  par((2 * dh, d)), par((r, 3 * d))],
        out_specs=full((r, d)),
        out_shape=jax.ShapeDtypeStruct((r, d), jnp.float32),
        name="sample_out",
    )(x, o, sza, oc, w_out, mod_s)


def kernel(x_prompt, x_sample, cache_k, cache_v, cache_logf, state_conv, page_table, c_prompt, c_sample,
           w_in, b_forget, w_out, norm_g, q_gain, k_gain, conv_w, conv_b, w_ada, b_ada):
    depth, d, _ = w_in.shape
    b, s, _ = x_prompt.shape
    n_seq = x_sample.shape[0]
    n_heads = b_forget.shape[-1]
    dh = n_heads * HEAD_DIM
    n_phys, page = cache_k.shape[1], cache_k.shape[2]
    tm = min(512, s)
    tq = tk = min(512, s)
    r_pad = -(-n_seq // SUBLANES) * SUBLANES

    f0 = 4 * dh
    w_t = jnp.swapaxes(w_in, 1, 2)
    w_main = _repack_w_in(w_t, dh, n_heads)
    w_f = _bf(jnp.pad(w_in[:, :, f0:f0 + n_heads], ((0, 0), (0, 0), (0, LANES - n_heads))))
    b_f = jnp.pad(b_forget, ((0, 0), (0, LANES - n_heads))).reshape(depth, 1, LANES)
    w_out_b = _bf(w_out)
    qg = jnp.tile(q_gain, (1, n_heads)).reshape(depth, 1, dh)
    kg = jnp.tile(k_gain, (1, n_heads)).reshape(depth, 1, dh)
    g3 = norm_g.reshape(depth, 1, d)
    cb3 = conv_b.reshape(depth, 1, dh)
    half = dh // 2
    hid = jnp.arange(half) // HEAD_DIM
    bd = _bf(jnp.where(hid[:, None] == hid[None, :], 1.0 / HEAD_DIM, 0.0))

    n_prompt = c_prompt.shape[0]
    c_all = jnp.concatenate([c_prompt, c_sample], axis=0)
    rows = c_all.shape[0]
    rows_pad = -(-rows // SUBLANES) * SUBLANES
    c_all = jnp.pad(c_all, ((0, rows_pad - rows), (0, 0)))
    mods = _ada(c_all, w_ada, b_ada)
    mods4 = mods.reshape(depth, rows_pad, 1, 3 * d)
    mod_s = jnp.pad(mods[:, n_prompt:n_prompt + n_seq, :], ((0, 0), (0, r_pad - n_seq), (0, 0)))

    ck = jnp.transpose(cache_k, (0, 1, 3, 4, 2)).reshape(depth * n_phys, n_heads, HEAD_DIM, page)
    cv = jnp.transpose(cache_v, (0, 1, 3, 4, 2)).reshape(depth * n_phys, n_heads, HEAD_DIM, page)
    clf = jnp.transpose(cache_logf, (0, 1, 3, 2)).reshape(depth * n_phys, n_heads, page)
    st_all = jnp.pad(state_conv.reshape(depth, n_seq, (CONV_W - 1) * dh),
                     ((0, 0), (0, r_pad - n_seq), (0, 0)))

    yp = x_prompt
    ys = jnp.pad(x_sample.reshape(n_seq, d), ((0, r_pad - n_seq), (0, 0)))
    lp, cp = [], []
    ksm, vsm, lsm, csm = [], [], [], []
    kv_all = ()
    for l in range(depth):
        qb, kb, vtb, kt_all, vt_all, lft, sza, oc, hist = _prompt_in(
            l, yp, mods4, g3, w_main, w_f, b_f, qg, kg, conv_w, cb3, bd, tm, tk, kv_all)
        kv_all = (kt_all, vt_all)
        ka = _cumsum(lft)
        q_s, k_s, v_s, lf_s, sza_s, oc_s, hist_s = _sample_in(
            l, ys, mod_s, st_all, g3, w_main, w_f, b_f, qg, kg, conv_w, cb3, bd)
        oa, o_s = _attn(qb, kb, ka, vtb, sza, tq, tk,
                        l * n_phys, page_table, q_s, k_s, v_s, lf_s, ck, cv, clf, n_seq)
        yp = _prompt_out(l, yp, oa, oc, w_out_b, mods4, min(1024, s))
        ys = _sample_out(l, ys, o_s, sza_s, oc_s, w_out_b, mod_s)
        lp.append(jnp.transpose(lft, (0, 2, 1)))
        cp.append(hist)
        ksm.append(k_s[:n_seq].reshape(n_seq, 1, n_heads, HEAD_DIM))
        vsm.append(v_s[:n_seq].reshape(n_seq, 1, n_heads, HEAD_DIM))
        lsm.append(lf_s[:n_seq, :n_heads].reshape(n_seq, 1, n_heads))
        csm.append(hist_s[:n_seq].reshape(n_seq, CONV_W - 1, dh))
    k_prompt, v_prompt = (
        jnp.transpose(a.reshape(depth, b, n_heads, HEAD_DIM, s), (0, 1, 4, 2, 3)) for a in kv_all)

    return (yp, ys[:n_seq].reshape(n_seq, 1, d), k_prompt, v_prompt, jnp.stack(lp),
            jnp.stack(cp), jnp.stack(ksm), jnp.stack(vsm), jnp.stack(lsm), jnp.stack(csm))
```

```python
import functools

import jax
import jax.numpy as jnp
from jax import lax
from jax.experimental import pallas as pl
from jax.experimental.pallas import tpu as pltpu

HEAD_DIM = 64
CONV_W = 3
RMS_EPS = 1e-6
LANES = 128
SUBLANES = 8
BF16_ROWS = 16
LOG2E = 1.4426950408889634
NEG_INF = float("-inf")
PAGES_PER_CHUNK = 16
PAGE_RING = 3
EXP_ROWS = 128
N_PROMPT_IN_INPUTS = 11

_NT = (((1,), (1,)), ((), ()))


def _bf(x):
    return x.astype(jnp.bfloat16)


def _dot(a, b):
    return jnp.dot(a, b, preferred_element_type=jnp.float32)


def _dot_nt(a, b):
    return lax.dot_general(a, b, _NT, preferred_element_type=jnp.float32)


def _silu(x):
    return x * (1.0 / (1.0 + jnp.exp(-x)))


def _log_sigmoid(z):
    return jnp.minimum(z, 0.0) - jnp.log(1.0 + jnp.exp(-jnp.abs(z)))


def _prefix_sum_lanes(x):
    n = x.shape[-1]
    lane = lax.broadcasted_iota(jnp.int32, x.shape, x.ndim - 1)
    sh = 1
    while sh < n:
        x = x + jnp.where(lane >= sh, pltpu.roll(x, sh, x.ndim - 1), 0.0)
        sh *= 2
    return x


def _ada_kernel(c_ref, w_ref, b_ref, o_ref):
    c = c_ref[...]
    o_ref[...] = _dot(_bf(_silu(c)), _bf(w_ref[...])) + b_ref[...]


def _ada(c_all, w_ada, b_ada):
    depth, d, n3 = w_ada.shape
    rows = c_all.shape[0]
    bn = 768
    return pl.pallas_call(
        _ada_kernel,
        grid=(depth, n3 // bn),
        in_specs=[
            pl.BlockSpec((rows, d), lambda l, j: (0, 0)),
            pl.BlockSpec((None, d, bn), lambda l, j: (l, 0, j)),
            pl.BlockSpec((None, 1, bn), lambda l, j: (l, 0, j)),
        ],
        out_specs=pl.BlockSpec((None, rows, bn), lambda l, j: (l, 0, j)),
        out_shape=jax.ShapeDtypeStruct((depth, rows, n3), jnp.float32),
        name="ada_mod",
    )(c_all, w_ada, b_ada.reshape(depth, 1, n3))


def _repack_kernel(wt_ref, o_ref):
    o_ref[...] = _bf(wt_ref[0].T)


def _repack_w_in(w_t, dh, n_skip):
    depth, _, d = w_t.shape
    n_groups = 8

    def src_rows(l, c):
        row = c * dh + jnp.where(c >= n_groups // 2, n_skip, 0)
        return (l, pl.multiple_of(row, SUBLANES), 0)

    return pl.pallas_call(
        _repack_kernel,
        grid=(depth, n_groups),
        in_specs=[pl.BlockSpec((pl.Element(1), pl.Element(dh), pl.Element(d)), src_rows)],
        out_specs=pl.BlockSpec((None, d, dh), lambda l, c: (l, 0, c)),
        out_shape=jax.ShapeDtypeStruct((depth, d, n_groups * dh), jnp.bfloat16),
        name="repack_w_in",
    )(w_t)


def _inproj_math(x, shift, scale, g, w_ref, wf_ref, bf_row, qg, kg, bd):
    dh = qg.shape[-1]
    ms = jnp.mean(x * x, axis=-1, keepdims=True)
    xn = x * lax.rsqrt(ms + RMS_EPS)
    h = _bf(xn * g * (1.0 + scale) + shift)

    def col(c):
        return _dot(h, w_ref[:, c * dh:(c + 1) * dh])

    def head_norm(t, gain):
        t2 = _bf(t * t)
        half = dh // 2
        msq = jnp.concatenate(
            [_dot(t2[:, :half], bd), _dot(t2[:, half:], bd)], axis=1)
        return t * lax.rsqrt(msq + RMS_EPS) * gain

    q = head_norm(col(0), qg)
    k = head_norm(col(1), kg)
    v = col(2)
    sza = _silu(col(3))
    fl = _dot(h, wf_ref[...]) + bf_row
    logf = _log_sigmoid(fl)
    bc = col(4)
    cu = col(5) * col(6)
    zc = col(7)
    return q, k, v, sza, logf, bc, cu, zc


def _prompt_in_kernel(*refs, d_model, n_heads, tk, n_alias):
    (x_ref, mod_ref, g_ref, w_ref, wf_ref, bf_ref, qg_ref, kg_ref,
     cw_ref, cb_ref, bd_ref) = refs[:N_PROMPT_IN_INPUTS]
    (qb_ref, kb_ref, vtb_ref, kt_ref, vt_ref, lft_ref, sza_ref, oc_ref,
     hist_ref, carry_ref) = refs[N_PROMPT_IN_INPUTS + n_alias:]
    i = pl.program_id(1)
    tm = x_ref.shape[0]
    mod = mod_ref[...]
    shift = mod[:, :d_model]
    scale = mod[:, d_model:2 * d_model]
    q, k, v, sza, logf, bc, cu, zc = _inproj_math(
        x_ref[...], shift, scale, g_ref[...], w_ref, wf_ref, bf_ref[...],
        qg_ref[...], kg_ref[...], bd_ref[...])

    qb_ref[...] = _bf(q * (LOG2E * HEAD_DIM ** -0.5))
    kb_ref[...] = _bf(k)
    k_t = k.T
    v_t = v.T
    if n_alias:
        kt_ref[...] = k_t
        vt_ref[...] = v_t
    else:
        for slot in range(kt_ref.shape[0]):
            kt_ref[slot] = k_t
            vt_ref[slot] = v_t
    for c in range(tm // tk):
        vtb_ref[c] = _bf(v_t[:, c * tk:(c + 1) * tk])
    sza_ref[...] = sza
    lft_ref[...] = logf.T[:n_heads, :]

    @pl.when(i == 0)
    def _():
        carry_ref[...] = jnp.zeros_like(carry_ref)

    prev = carry_ref[...]
    row = lax.broadcasted_iota(jnp.int32, cu.shape, 0)
    s1 = pltpu.roll(cu, 1, 0)
    s1 = jnp.where(row == 0, prev[7:8, :], s1)
    s2 = pltpu.roll(cu, 2, 0)
    s2 = jnp.where(row == 0, prev[6:7, :], jnp.where(row == 1, prev[7:8, :], s2))
    cw = cw_ref[...]
    conv = cb_ref[...] + s2 * cw[0:1, :] + s1 * cw[1:2, :] + cu * cw[2:3, :]
    oc_ref[...] = _bf(bc * conv * _silu(zc))
    carry_ref[...] = cu[tm - SUBLANES:, :]

    @pl.when(i == pl.num_programs(1) - 1)
    def _():
        hist_ref[...] = cu[tm - (CONV_W - 1):, :]


def _prompt_in(l, x, mods, norm_g, w_main, w_f, b_f, qg, kg, conv_w, conv_b, bd, tm, tk, kv_prev):
    depth = w_main.shape[0]
    b, s, d = x.shape
    dh = qg.shape[-1]
    n_heads = dh // HEAD_DIM
    nk = s // tk
    grid = (b, s // tm)
    row_spec = lambda width: pl.BlockSpec((None, tm, width), lambda bi, i: (bi, i, 0))
    col_spec = lambda height: pl.BlockSpec((None, height, tm), lambda bi, i: (bi, 0, i))
    if kv_prev:
        stacked_spec = pl.BlockSpec((None, None, dh, tm), lambda bi, i: (l, bi, 0, i))
    else:
        stacked_spec = pl.BlockSpec((depth, None, dh, tm), lambda bi, i: (0, bi, 0, i))
    par = lambda shape: pl.BlockSpec((None,) + shape, lambda bi, i: (l,) + (0,) * len(shape))
    n_alias = len(kv_prev)
    kernel = functools.partial(_prompt_in_kernel, d_model=d, n_heads=n_heads, tk=tk, n_alias=n_alias)
    return pl.pallas_call(
        kernel,
        grid=grid,
        input_output_aliases={N_PROMPT_IN_INPUTS + a: 3 + a for a in range(n_alias)},
        in_specs=[
            row_spec(d),
            pl.BlockSpec((None, None, 1, 3 * d), lambda bi, i: (l, bi, 0, 0)),
            par((1, d)),
            pl.BlockSpec((None, d, 8 * dh), lambda bi, i: (l, 0, 0),
                         pipeline_mode=pl.Buffered(1)),
            par((d, LANES)),
            par((1, LANES)),
            par((1, dh)),
            par((1, dh)),
            par((CONV_W, dh)),
            par((1, dh)),
            pl.BlockSpec(bd.shape, lambda bi, i: (0, 0)),
        ] + [pl.BlockSpec(memory_space=pl.ANY)] * n_alias,
        out_specs=[
            row_spec(dh),
            row_spec(dh),
            pl.BlockSpec((None, tm // tk, dh, tk), lambda bi, i: (bi, i, 0, 0)),
            stacked_spec,
            stacked_spec,
            col_spec(n_heads),
            row_spec(dh),
            row_spec(dh),
            pl.BlockSpec((None, CONV_W - 1, dh), lambda bi, i: (bi, 0, 0)),
        ],
        out_shape=[
            jax.ShapeDtypeStruct((b, s, dh), jnp.bfloat16),
            jax.ShapeDtypeStruct((b, s, dh), jnp.bfloat16),
            jax.ShapeDtypeStruct((b, nk, dh, tk), jnp.bfloat16),
            jax.ShapeDtypeStruct((depth, b, dh, s), jnp.float32),
            jax.ShapeDtypeStruct((depth, b, dh, s), jnp.float32),
            jax.ShapeDtypeStruct((b, n_heads, s), jnp.float32),
            jax.ShapeDtypeStruct((b, s, dh), jnp.float32),
            jax.ShapeDtypeStruct((b, s, dh), jnp.bfloat16),
            jax.ShapeDtypeStruct((b, CONV_W - 1, dh), jnp.float32),
        ],
        scratch_shapes=[pltpu.VMEM((SUBLANES, dh), jnp.float32)],
        compiler_params=pltpu.CompilerParams(
            dimension_semantics=("arbitrary", "arbitrary"),
            vmem_limit_bytes=56 * 1024 * 1024),
        name="prompt_in",
    )(x, mods, norm_g, w_main, w_f, b_f, qg, kg, conv_w, conv_b, bd, *kv_prev)


def _cumsum_kernel(lf_ref, ka_ref):
    n_heads, s = lf_ref.shape
    x = -_prefix_sum_lanes(lf_ref[...] * LOG2E)
    hi = _bf(x).astype(jnp.float32)
    r = x - hi
    mid = _bf(r).astype(jnp.float32)
    lo = r - mid
    pad = jnp.zeros((LANES - 3 * n_heads, s), jnp.float32)
    ka_ref[...] = _bf(jnp.concatenate([hi, mid, lo, pad], axis=0).T)


def _cumsum(lf_t):
    b, h, s = lf_t.shape
    return pl.pallas_call(
        _cumsum_kernel,
        grid=(b,),
        in_specs=[pl.BlockSpec((None, h, s), lambda bi: (bi, 0, 0))],
        out_specs=pl.BlockSpec((None, s, LANES), lambda bi: (bi, 0, 0)),
        out_shape=jax.ShapeDtypeStruct((b, s, LANES), jnp.bfloat16),
        name="forget_cumsum",
    )(lf_t)


def _attn_kernel(pt_ref, q_ref, k_ref, ka_ref, vt_ref, sza_ref,
                 qs_ref, kn_ref, vn_ref, lfn_ref, ck_hbm, cv_hbm, clf_hbm,
                 o_ref, os_ref, s_scr, tmax_scr, m_scr, acc_scr, *sample_scratch,
                 tq, tk, n_heads, n_chunks, every_block_has_chunk, sample_params):
    nq = q_ref.shape[0] // tq
    ones_rows = jnp.ones((BF16_ROWS, tk), jnp.bfloat16)
    lane = lax.broadcasted_iota(jnp.int32, (tq, LANES), 1)
    bias_ones = []
    for hh in range(2):
        head = 2 * pl.program_id(1) + hh
        mine = (lane < 3 * n_heads) & (lax.rem(lane, n_heads) == head)
        bias_ones.append(_bf(jnp.where(mine, 1.0, 0.0)))

    def q_rows(qb):
        return pl.ds(pl.multiple_of(qb * tq, tq), tq)

    def q_operands(qb):
        q = q_ref[q_rows(qb), :]
        zero = jnp.zeros_like(q)
        q_split = (jnp.where(lane < HEAD_DIM, q, zero), jnp.where(lane >= HEAD_DIM, q, zero))
        return [jnp.concatenate([q_split[hh], bias_ones[hh]], axis=1) for hh in range(2)]

    def qk(q_heads, j):
        rows = pl.ds(pl.multiple_of(j * tk, tk), tk)
        k_t = jnp.concatenate([k_ref[rows, :], ka_ref[rows, :]], axis=1)
        return [_dot_nt(k_t, q_heads[hh]) for hh in range(2)]

    def scores_to_scratch(s_pair, masked, slot):
        for hh in range(2):
            s_t = s_pair[hh]
            if masked:
                kpos = lax.broadcasted_iota(jnp.int32, s_t.shape, 0)
                qpos = lax.broadcasted_iota(jnp.int32, s_t.shape, 1)
                s_t = jnp.where(kpos <= qpos, s_t, NEG_INF)
            s_scr[slot, hh] = s_t
            col_max = jnp.max(s_t, axis=0, keepdims=True)
            tmax_scr[slot, hh] = jnp.broadcast_to(col_max, (SUBLANES, tq))

    def rep(stat, rows):
        return jnp.broadcast_to(stat[None], (rows // SUBLANES, SUBLANES, tq)).reshape(rows, tq)

    def exps(slot):
        out = []
        for hh in range(2):
            m_old = m_scr[hh]
            m_new = jnp.maximum(m_old, tmax_scr[slot, hh])
            alpha = jnp.exp2(m_old - m_new)
            m_scr[hh] = m_new
            m_rep = rep(m_new, EXP_ROWS)
            p = jnp.concatenate(
                [_bf(jnp.exp2(s_scr[slot, hh, c * EXP_ROWS:(c + 1) * EXP_ROWS, :] - m_rep))
                 for c in range(tk // EXP_ROWS)], axis=0)
            out.append((alpha, p))
        return out

    def pv(j, ap):
        out = []
        for hh in range(2):
            alpha, p = ap[hh]
            v_t = jnp.concatenate(
                [vt_ref[j, hh * HEAD_DIM:(hh + 1) * HEAD_DIM, :], ones_rows], axis=0)
            out.append((alpha, _dot(v_t, p)))
        return out

    def accumulate(av):
        for hh in range(2):
            alpha, pv_t = av[hh]
            acc_scr[hh] = rep(alpha, pv_t.shape[0]) * acc_scr[hh] + pv_t

    def reset_stats():
        m_scr[...] = jnp.full_like(m_scr, NEG_INF)
        acc_scr[...] = jnp.zeros_like(acc_scr)

    scores_to_scratch(qk(q_operands(0), 0), True, 0)
    reset_stats()

    def q_block(qi, carry):
        chunk = (pl.program_id(0) * pl.num_programs(1) + pl.program_id(1)) * nq + qi

        def sample(stages):
            _sample_chunk(chunk, n_chunks, pt_ref, qs_ref, kn_ref, vn_ref, lfn_ref,
                          ck_hbm, cv_hbm, clf_hbm, os_ref, *sample_scratch,
                          stages=stages, **sample_params)

        if every_block_has_chunk:
            sample(("fetch",))
        else:
            @pl.when(chunk < n_chunks)
            def _():
                sample(("fetch", "fold"))

        q_heads = q_operands(qi)

        def step(j_next, slot_next):
            ap = exps(1 - slot_next)
            s_next = qk(q_heads, j_next)
            av = pv(j_next - 1, ap)
            scores_to_scratch(s_next, False, slot_next)
            accumulate(av)

        n_plain = jnp.maximum(qi - 1, 0)

        def body(jj, c):
            step(2 * jj + 1, 1)
            step(2 * jj + 2, 0)
            return c

        lax.fori_loop(0, n_plain // 2, body, 0)

        @pl.when(lax.rem(n_plain, 2) == 1)
        def _():
            step(qi - 1, 1)

        def close(slot, with_diagonal):
            if with_diagonal:
                ap = exps(1 - slot)
                s_diag = qk(q_heads, qi)
            s_first = qk(q_operands(jnp.minimum(qi + 1, nq - 1)), 0)
            if with_diagonal:
                av = pv(qi - 1, ap)
                scores_to_scratch(s_diag, True, slot)
                accumulate(av)
            accumulate(pv(qi, exps(slot)))
            o_t = jnp.concatenate(
                [acc_scr[hh, :HEAD_DIM, :] / acc_scr[hh, HEAD_DIM:HEAD_DIM + 1, :]
                 for hh in range(2)], axis=0)
            o_ref[q_rows(qi), :] = _bf(o_t.T * sza_ref[q_rows(qi), :])
            scores_to_scratch(s_first, False, 0)
            reset_stats()
            if every_block_has_chunk:
                sample(("fold",))

        @pl.when(qi == 0)
        def _():
            close(0, False)

        for parity in range(2):
            @pl.when((qi > 0) & (lax.rem(qi, 2) == parity))
            def _():
                close(parity, True)

        return carry

    lax.fori_loop(0, nq, q_block, 0)


def _attn(qb, kb, ka, vtb, sza, tq, tk, layer_off, page_table, qs, kn, vn, lfn, ck, cv, clf, n_seq):
    b, s, dh = qb.shape
    npair = dh // LANES
    nk = s // tk
    n_heads = dh // HEAD_DIM
    r = qs.shape[0]
    n_pages = page_table.shape[1]
    cpp = min(PAGES_PER_CHUNK, n_pages)
    n_chunks = n_seq * (n_pages // cpp)
    assert n_chunks <= b * npair * (s // tq), "one sample chunk per prompt query block"
    kernel = functools.partial(
        _attn_kernel, tq=tq, tk=tk, n_heads=n_heads, n_chunks=n_chunks,
        every_block_has_chunk=(n_chunks == b * npair * (s // tq)),
        sample_params=dict(layer_off=layer_off, n_pages=n_pages, cpp=cpp))
    seq_spec = pl.BlockSpec((None, s, LANES), lambda bi, hp, pt: (bi, 0, hp))
    full = lambda shape: pl.BlockSpec(shape, lambda bi, hp, pt: (0,) * len(shape))
    anyspec = pl.BlockSpec(memory_space=pl.ANY)
    grid_spec = pltpu.PrefetchScalarGridSpec(
        num_scalar_prefetch=1,
        grid=(b, npair),
        in_specs=[
            seq_spec,
            seq_spec,
            pl.BlockSpec((None, s, LANES), lambda bi, hp, pt: (bi, 0, 0)),
            pl.BlockSpec((None, nk, LANES, tk), lambda bi, hp, pt: (bi, 0, hp, 0)),
            seq_spec,
            full((r, dh)), full((r, dh)), full((r, dh)), full((r, LANES)),
            anyspec, anyspec, anyspec,
        ],
        out_specs=[seq_spec, full((r, dh))],
        scratch_shapes=[
            pltpu.VMEM((2, 2, tk, tq), jnp.float32),
            pltpu.VMEM((2, 2, SUBLANES, tq), jnp.float32),
            pltpu.VMEM((2, SUBLANES, tq), jnp.float32),
            pltpu.VMEM((2, HEAD_DIM + BF16_ROWS, tq), jnp.float32),
        ] + _sample_scratch(n_heads, dh, cpp, ck.shape[-1]),
    )
    return pl.pallas_call(
        kernel,
        grid_spec=grid_spec,
        out_shape=[jax.ShapeDtypeStruct((b, s, dh), jnp.bfloat16),
                   jax.ShapeDtypeStruct((r, dh), jnp.float32)],
        compiler_params=pltpu.CompilerParams(
            dimension_semantics=("arbitrary", "arbitrary"),
            vmem_limit_bytes=56 * 1024 * 1024),
        name="attn",
    )(page_table.reshape(-1), qb, kb, ka, vtb, sza, qs, kn, vn, lfn, ck, cv, clf)


def _out_kernel(x_ref, oa_ref, oc_ref, w_ref, gate_ref, y_ref):
    dh = oa_ref.shape[-1]
    y = _dot(oa_ref[...], w_ref[:dh, :]) + _dot(oc_ref[...], w_ref[dh:, :])
    y_ref[...] = x_ref[...] + gate_ref[...] * y


def _prompt_out(l, x, oa, oc, w_out, mods, tm):
    b, s, d = x.shape
    dh = oa.shape[-1]
    row_spec = lambda width: pl.BlockSpec((None, tm, width), lambda bi, i: (bi, i, 0))
    return pl.pallas_call(
        _out_kernel,
        grid=(b, s // tm),
        in_specs=[
            row_spec(d), row_spec(dh), row_spec(dh),
            pl.BlockSpec((None, 2 * dh, d), lambda bi, i: (l, 0, 0)),
            pl.BlockSpec((None, None, 1, d), lambda bi, i: (l, bi, 0, 2)),
        ],
        out_specs=row_spec(d),
        out_shape=jax.ShapeDtypeStruct((b, s, d), jnp.float32),
        compiler_params=pltpu.CompilerParams(
            dimension_semantics=("arbitrary", "arbitrary"),
            vmem_limit_bytes=48 * 1024 * 1024),
        name="prompt_out",
    )(x, oa, oc, w_out, mods)


def _sample_in_kernel(x_ref, mod_ref, st_ref, g_ref, w_ref, wf_ref, bf_ref, qg_ref, kg_ref,
                      cw_ref, cb_ref, bd_ref,
                      q_ref, k_ref, v_ref, lf_ref, sza_ref, oc_ref, hist_ref, *, d_model):
    mod = mod_ref[...]
    shift = mod[:, :d_model]
    scale = mod[:, d_model:2 * d_model]
    q, k, v, sza, logf, bc, cu, zc = _inproj_math(
        x_ref[...], shift, scale, g_ref[...], w_ref, wf_ref, bf_ref[...],
        qg_ref[...], kg_ref[...], bd_ref[...])
    dh = q.shape[-1]
    q_ref[...] = q * (LOG2E * HEAD_DIM ** -0.5)
    k_ref[...] = k
    v_ref[...] = v
    lf_ref[...] = logf
    sza_ref[...] = sza
    st = st_ref[...]
    h0 = st[:, :dh]
    h1 = st[:, dh:]
    cw = cw_ref[...]
    conv = cb_ref[...] + h0 * cw[0:1, :] + h1 * cw[1:2, :] + cu * cw[2:3, :]
    oc_ref[...] = _bf(bc * conv * _silu(zc))
    hist_ref[...] = jnp.concatenate([h1, cu], axis=1)


def _sample_in(l, x, mod_s, st, norm_g, w_main, w_f, b_f, qg, kg, conv_w, conv_b, bd):
    r, d = x.shape
    dh = qg.shape[-1]
    full = lambda shape: pl.BlockSpec(shape, lambda i: (0,) * len(shape))
    par = lambda shape: pl.BlockSpec((None,) + shape, lambda i: (l,) + (0,) * len(shape))
    kernel = functools.partial(_sample_in_kernel, d_model=d)
    f32 = jnp.float32
    return pl.pallas_call(
        kernel,
        grid=(1,),
        in_specs=[
            full((r, d)), par((r, 3 * d)), par((r, 2 * dh)), par((1, d)),
            par((d, 8 * dh)), par((d, LANES)), par((1, LANES)), par((1, dh)), par((1, dh)),
            par((CONV_W, dh)), par((1, dh)), full(bd.shape),
        ],
        out_specs=[
            full((r, dh)), full((r, dh)), full((r, dh)), full((r, LANES)), full((r, dh)),
            full((r, dh)), full((r, 2 * dh)),
        ],
        out_shape=[
            jax.ShapeDtypeStruct((r, dh), f32),
            jax.ShapeDtypeStruct((r, dh), f32),
            jax.ShapeDtypeStruct((r, dh), f32),
            jax.ShapeDtypeStruct((r, LANES), f32),
            jax.ShapeDtypeStruct((r, dh), f32),
            jax.ShapeDtypeStruct((r, dh), jnp.bfloat16),
            jax.ShapeDtypeStruct((r, 2 * dh), f32),
        ],
        compiler_params=pltpu.CompilerParams(vmem_limit_bytes=56 * 1024 * 1024),
        name="sample_in",
    )(x, mod_s, st, norm_g, w_main, w_f, b_f, qg, kg, conv_w, conv_b, bd)


def _sample_chunk(g, nsteps, pt_ref, q_ref, kn_ref, vn_ref, lfn_ref, k_hbm, v_hbm, lf_hbm,
                  o_ref, kbuf, vbuf, lfbuf, sems, acc_ref, m_ref, l_ref, c_ref,
                  *, layer_off, n_pages, cpp, stages=("fetch", "fold")):
    n_heads, hd, n_tok = kbuf.shape[1], kbuf.shape[2], kbuf.shape[3]
    dh = n_heads * hd
    page = n_tok // cpp
    nch = n_pages // cpp
    n_slots = kbuf.shape[0]
    ahead = n_slots - 1
    slot = lax.rem(g, n_slots)
    seq = g // nch
    ch = lax.rem(g, nch)

    def copies(step, sl):
        sq = step // nch
        c0 = lax.rem(step, nch) * cpp
        out = []
        for p in range(cpp):
            pid = pt_ref[sq * n_pages + c0 + p] + layer_off
            lanes = pl.ds(p * page, page)
            out.append(pltpu.make_async_copy(k_hbm.at[pid], kbuf.at[sl, :, :, lanes], sems.at[sl, 0]))
            out.append(pltpu.make_async_copy(v_hbm.at[pid], vbuf.at[sl, :, :, lanes], sems.at[sl, 1]))
            out.append(pltpu.make_async_copy(lf_hbm.at[pid], lfbuf.at[sl, :, lanes], sems.at[sl, 2]))
        return out

    if "fetch" in stages:
        @pl.when(g == 0)
        def _():
            o_ref[...] = jnp.zeros_like(o_ref)
            for first in range(min(ahead, nsteps)):
                for cp in copies(first, first):
                    cp.start()

        @pl.when(g + ahead < nsteps)
        def _():
            for cp in copies(g + ahead, lax.rem(g + ahead, n_slots)):
                cp.start()

        for cp in copies(g, slot):
            cp.wait()

        @pl.when(ch == 0)
        def _():
            acc_ref[...] = jnp.zeros_like(acc_ref)
            m_ref[...] = jnp.full_like(m_ref, NEG_INF)
            l_ref[...] = jnp.zeros_like(l_ref)
            c_ref[...] = jnp.zeros_like(c_ref)

    if "fold" not in stages:
        return

    qrow = q_ref[pl.ds(seq, 1), :]
    sub = lax.broadcasted_iota(jnp.int32, (BF16_ROWS, dh), 0)
    lane_head = lax.broadcasted_iota(jnp.int32, (BF16_ROWS, dh), 1) // hd
    diag = sub == lane_head
    qbd = jnp.where(diag, qrow, 0.0)

    kc = _bf(kbuf[slot].reshape(dh, n_tok))
    s = _dot(_bf(qbd), kc)[:n_heads]
    cum = _prefix_sum_lanes(lfbuf[slot] * LOG2E) + c_ref[...]
    c_ref[...] = cum[:, n_tok - 1:n_tok]
    logit = s - cum
    m_old = m_ref[...]
    m_new = jnp.maximum(m_old, jnp.max(logit, axis=1, keepdims=True))
    alpha = jnp.exp2(m_old - m_new)
    p_t = jnp.exp2(logit - m_new)
    l_ref[...] = alpha * l_ref[...] + jnp.sum(p_t, axis=1, keepdims=True)
    m_ref[...] = m_new
    p16 = _bf(jnp.concatenate([p_t, jnp.zeros_like(p_t)], axis=0))
    vc = _bf(vbuf[slot].reshape(dh, n_tok))
    acc_ref[...] = alpha * acc_ref[...] + _dot_nt(p16, vc)[:n_heads]

    @pl.when(ch == nch - 1)
    def _():
        krow = kn_ref[pl.ds(seq, 1), :]
        vrow = vn_ref[pl.ds(seq, 1), :]
        lfrow = lfn_ref[pl.ds(seq, 1), :]
        sub8 = lax.broadcasted_iota(jnp.int32, (n_heads, LANES), 0)
        lane8 = lax.broadcasted_iota(jnp.int32, (n_heads, LANES), 1)
        lfcol = jnp.sum(jnp.where(sub8 == lane8, lfrow, 0.0), axis=1, keepdims=True) * LOG2E
        s_self = jnp.sum(qbd[:n_heads] * krow, axis=1, keepdims=True) - (c_ref[...] + lfcol)
        m_prev = m_ref[...]
        m_fin = jnp.maximum(m_prev, s_self)
        a_fin = jnp.exp2(m_prev - m_fin)
        p_self = jnp.exp2(s_self - m_fin)
        l_fin = a_fin * l_ref[...] + p_self
        o = (a_fin * acc_ref[...] + p_self * vrow) / l_fin
        o_ref[pl.ds(seq, 1), :] = jnp.sum(jnp.where(diag[:n_heads], o, 0.0), axis=0, keepdims=True)


def _sample_scratch(n_heads, dh, cpp, page):
    return [
        pltpu.VMEM((PAGE_RING, n_heads, HEAD_DIM, cpp * page), jnp.float32),
        pltpu.VMEM((PAGE_RING, n_heads, HEAD_DIM, cpp * page), jnp.float32),
        pltpu.VMEM((PAGE_RING, n_heads, cpp * page), jnp.float32),
        pltpu.SemaphoreType.DMA((PAGE_RING, 3)),
        pltpu.VMEM((n_heads, dh), jnp.float32),
        pltpu.VMEM((n_heads, 1), jnp.float32),
        pltpu.VMEM((n_heads, 1), jnp.float32),
        pltpu.VMEM((n_heads, 1), jnp.float32),
    ]


def _sample_out_kernel(x_ref, o_ref, sza_ref, oc_ref, w_ref, mod_ref, y_ref, *, d_model):
    dh = oc_ref.shape[-1]
    oa = _bf(o_ref[...] * sza_ref[...])
    y = _dot(oa, w_ref[:dh, :]) + _dot(oc_ref[...], w_ref[dh:, :])
    gate = mod_ref[:, 2 * d_model:]
    y_ref[...] = x_ref[...] + gate * y


def _sample_out(l, x, o, sza, oc, w_out, mod_s):
    r, d = x.shape
    dh = oc.shape[-1]
    full = lambda shape: pl.BlockSpec(shape, lambda i: (0,) * len(shape))
    par = lambda shape: pl.BlockSpec((None,) + shape, lambda i: (l,) + (0,) * len(shape))
    return pl.pallas_call(
        functools.partial(_sample_out_kernel, d_model=d),
        grid=(1,),
        in_specs=[full((r, d)), full((r, dh)), full((r, dh)), full((r, dh)),
                  par((2 * dh, d)), par((r, 3 * d))],
        out_specs=full((r, d)),
        out_shape=jax.ShapeDtypeStruct((r, d), jnp.float32),
        name="sample_out",
    )(x, o, sza, oc, w_out, mod_s)


def kernel(x_prompt, x_sample, cache_k, cache_v, cache_logf, state_conv, page_table, c_prompt, c_sample,
           w_in, b_forget, w_out, norm_g, q_gain, k_gain, conv_w, conv_b, w_ada, b_ada):
    depth, d, _ = w_in.shape
    b, s, _ = x_prompt.shape
    n_seq = x_sample.shape[0]
    n_heads = b_forget.shape[-1]
    dh = n_heads * HEAD_DIM
    n_phys, page = cache_k.shape[1], cache_k.shape[2]
    tm = min(512, s)
    tq = tk = min(512, s)
    r_pad = -(-n_seq // SUBLANES) * SUBLANES

    f0 = 4 * dh
    w_t = jnp.swapaxes(w_in, 1, 2)
    w_main = _repack_w_in(w_t, dh, n_heads)
    w_f = _bf(jnp.pad(w_in[:, :, f0:f0 + n_heads], ((0, 0), (0, 0), (0, LANES - n_heads))))
    b_f = jnp.pad(b_forget, ((0, 0), (0, LANES - n_heads))).reshape(depth, 1, LANES)
    w_out_b = _bf(w_out)
    qg = jnp.tile(q_gain, (1, n_heads)).reshape(depth, 1, dh)
    kg = jnp.tile(k_gain, (1, n_heads)).reshape(depth, 1, dh)
    g3 = norm_g.reshape(depth, 1, d)
    cb3 = conv_b.reshape(depth, 1, dh)
    half = dh // 2
    hid = jnp.arange(half) // HEAD_DIM
    bd = _bf(jnp.where(hid[:, None] == hid[None, :], 1.0 / HEAD_DIM, 0.0))

    n_prompt = c_prompt.shape[0]
    c_all = jnp.concatenate([c_prompt, c_sample], axis=0)
    rows = c_all.shape[0]
    rows_pad = -(-rows // SUBLANES) * SUBLANES
    c_all = jnp.pad(c_all, ((0, rows_pad - rows), (0, 0)))
    mods = _ada(c_all, w_ada, b_ada)
    mods4 = mods.reshape(depth, rows_pad, 1, 3 * d)
    mod_s = jnp.pad(mods[:, n_prompt:n_prompt + n_seq, :], ((0, 0), (0, r_pad - n_seq), (0, 0)))

    ck = jnp.transpose(cache_k, (0, 1, 3, 4, 2)).reshape(depth * n_phys, n_heads, HEAD_DIM, page)
    cv = jnp.transpose(cache_v, (0, 1, 3, 4, 2)).reshape(depth * n_phys, n_heads, HEAD_DIM, page)
    clf = jnp.transpose(cache_logf, (0, 1, 3, 2)).reshape(depth * n_phys, n_heads, page)
    st_all = jnp.pad(state_conv.reshape(depth, n_seq, (CONV_W - 1) * dh),
                     ((0, 0), (0, r_pad - n_seq), (0, 0)))

    yp = x_prompt
    ys = jnp.pad(x_sample.reshape(n_seq, d), ((0, r_pad - n_seq), (0, 0)))
    lp, cp = [], []
    ksm, vsm, lsm, csm = [], [], [], []
    kv_all = ()
    for l in range(depth):
        qb, kb, vtb, kt_all, vt_all, lft, sza, oc, hist = _prompt_in(
            l, yp, mods4, g3, w_main, w_f, b_f, qg, kg, conv_w, cb3, bd, tm, tk, kv_all)
        kv_all = (kt_all, vt_all)
        ka = _cumsum(lft)
        q_s, k_s, v_s, lf_s, sza_s, oc_s, hist_s = _sample_in(
            l, ys, mod_s, st_all, g3, w_main, w_f, b_f, qg, kg, conv_w, cb3, bd)
        oa, o_s = _attn(qb, kb, ka, vtb, sza, tq, tk,
                        l * n_phys, page_table, q_s, k_s, v_s, lf_s, ck, cv, clf, n_seq)
        yp = _prompt_out(l, yp, oa, oc, w_out_b, mods4, min(1024, s))
        ys = _sample_out(l, ys, o_s, sza_s, oc_s, w_out_b, mod_s)
        lp.append(jnp.transpose(lft, (0, 2, 1)))
        cp.append(hist)
        ksm.append(k_s[:n_seq].reshape(n_seq, 1, n_heads, HEAD_DIM))
        vsm.append(v_s[:n_seq].reshape(n_seq, 1, n_heads, HEAD_DIM))
        lsm.append(lf_s[:n_seq, :n_heads].reshape(n_seq, 1, n_heads))
        csm.append(hist_s[:n_seq].reshape(n_seq, CONV_W - 1, dh))
    k_prompt, v_prompt = (
        jnp.transpose(a.reshape(depth, b, n_heads, HEAD_DIM, s), (0, 1, 4, 2, 3)) for a in kv_all)

    return (yp, ys[:n_seq].reshape(n_seq, 1, d), k_prompt, v_prompt, jnp.stack(lp),
            jnp.stack(cp), jnp.stack(ksm), jnp.stack(vsm), jnp.stack(lsm), jnp.stack(csm))
```
